```python
import math
import jax, jax.numpy as jnp
from jax import lax
import numpy as np

D_MODEL = 1024
BATCH = 16
SEQ = 2048
DEPTH = 2

GRID_W = 64
CTX_LEN = 256
N_MIXERS = 2
N_A_LAYERS = (DEPTH + 1) // 2
N_B_LAYERS = DEPTH // 2
CONV_WIDTH = 3
DIFF_HEADS = 8
DIFF_HEAD_DIM = 64
DIFF_V_DIM = 2 * DIFF_HEAD_DIM
ROPE_BASE = 10000.0
N_EXPERTS = 16
EC_FACTOR = 2
EXPERT_FF = 2048
Q_BLOCK = 128
N_MOD = 6
EPS = 1e-6

kernel_name = "hybrid_shortconv_diffattn_ecmoe_dit"


def rmsnorm(x, g):
    xf = x.astype(jnp.float32)
    xf = xf * lax.rsqrt(jnp.mean(xf * xf, axis=-1, keepdims=True) + EPS)
    return xf.astype(x.dtype) * g


def modulate(h, shift, scale):
    return h * (1.0 + scale) + shift


def ada_mod(cond, w, b):
    return jnp.split(jax.nn.silu(cond) @ w + b, N_MOD, axis=-1)


def axial_rope_angles(n):
    rows = n // GRID_W
    row = jnp.repeat(jnp.arange(rows), GRID_W).astype(jnp.float32)
    col = jnp.tile(jnp.arange(GRID_W), rows).astype(jnp.float32)
    n_freq = DIFF_HEAD_DIM // 4
    inv = 1.0 / (ROPE_BASE ** (jnp.arange(n_freq, dtype=jnp.float32) / n_freq))
    return row[:, None] * inv, col[:, None] * inv


def rope_rotate(x, ang):
    x1, x2 = jnp.split(x, 2, axis=-1)
    cos = jnp.cos(ang).astype(x.dtype)
    sin = jnp.sin(ang).astype(x.dtype)
    return jnp.concatenate([x1 * cos - x2 * sin, x2 * cos + x1 * sin], axis=-1)


def apply_axial_rope(x, ang_row, ang_col):
    xr, xc = jnp.split(x, 2, axis=-1)
    ar = ang_row[:, None, None, :]
    ac = ang_col[:, None, None, :]
    return jnp.concatenate([rope_rotate(xr, ar), rope_rotate(xc, ac)], axis=-1)


def short_conv_mixer(h, w_in, k_conv, w_out):
    n = h.shape[1]
    b_gate, c_gate, v = jnp.split(h @ w_in, 3, axis=-1)
    u = c_gate * v
    pad = (CONV_WIDTH - 1) // 2
    up = jnp.pad(u, ((0, 0), (pad, pad), (0, 0)))
    conv = sum(k_conv[j] * up[:, j:j + n] for j in range(CONV_WIDTH))
    return (b_gate * conv) @ w_out


def heads_qk(t):
    return t.reshape(t.shape[0], t.shape[1], DIFF_HEADS, 2, DIFF_HEAD_DIM)


def heads_v(t):
    return t.reshape(t.shape[0], t.shape[1], DIFF_HEADS, DIFF_V_DIM)


def diff_attend(q, k, v, lam):
    scale = DIFF_HEAD_DIM ** -0.5
    s = jnp.einsum('bqhcd,bkhcd->cbhqk', q, k).astype(jnp.float32) * scale
    p = jax.nn.softmax(s, axis=-1)
    a = p[0] - lam * p[1]
    return jnp.einsum('bhqk,bkhe->bqhe', a.astype(v.dtype), v)


def diff_attn_mixer(h_lat, h_ctx, w_qkv, lq1, lk1, lq2, lk2, subln_g, w_o,
                    layer_idx, ang_row, ang_col, need_ctx):
    lam_init = 0.8 - 0.6 * math.exp(-0.3 * layer_idx)
    lam = (jnp.exp(jnp.sum(lq1.astype(jnp.float32) * lk1.astype(jnp.float32)))
           - jnp.exp(jnp.sum(lq2.astype(jnp.float32) * lk2.astype(jnp.float32)))
           + lam_init)
    bsz, n = h_lat.shape[0], h_lat.shape[1]
    q_l, k_l, v_l = jnp.split(h_lat @ w_qkv, 3, axis=-1)
    q_l = apply_axial_rope(heads_qk(q_l), ang_row, ang_col)
    k_l = apply_axial_rope(heads_qk(k_l), ang_row, ang_col)
    v_l = heads_v(v_l)
    k_c, v_c = jnp.split(h_ctx @ w_qkv[:, D_MODEL:], 2, axis=-1)
    k_c, v_c = heads_qk(k_c), heads_v(v_c)
    k_all = jnp.concatenate([k_c, k_l], axis=1)
    v_all = jnp.concatenate([v_c, v_l], axis=1)

    def finish(o):
        o = rmsnorm(o, subln_g) * (1.0 - lam_init)
        return o.reshape(o.shape[0], o.shape[1], D_MODEL) @ w_o

    nblk = n // Q_BLOCK
    qb = jnp.swapaxes(q_l.reshape(bsz, nblk, Q_BLOCK, DIFF_HEADS, 2, DIFF_HEAD_DIM), 0, 1)
    o_l = lax.map(lambda qq: diff_attend(qq, k_all, v_all, lam), qb)
    o_l = jnp.swapaxes(o_l, 0, 1).reshape(bsz, n, DIFF_HEADS, DIFF_V_DIM)
    out_l = finish(o_l)
    out_c = None
    if need_ctx:
        q_c = heads_qk(h_ctx @ w_qkv[:, :D_MODEL])
        out_c = finish(diff_attend(q_c, k_c, v_c, lam))
    return out_l, out_c


def ec_moe(h, w_router, w_gate, w_up, w_down):
    bsz, n, d = h.shape
    cap = EC_FACTOR * n // N_EXPERTS
    aff = jax.nn.softmax((h @ w_router).astype(jnp.float32), axis=-1)
    vals, idx = lax.top_k(jnp.swapaxes(aff, 1, 2), cap)
    idx_flat = idx.reshape(bsz, N_EXPERTS * cap)
    xs = jnp.take_along_axis(h, idx_flat[..., None], axis=1).reshape(bsz, N_EXPERTS, cap, d)
    g = jnp.einsum('becd,edf->becf', xs, w_gate)
    u = jnp.einsum('becd,edf->becf', xs, w_up)
    y = jnp.einsum('becf,efd->becd', jax.nn.silu(g) * u, w_down)
    y = y * vals[..., None].astype(y.dtype)
    out = jnp.zeros_like(h)
    return out.at[jnp.arange(bsz)[:, None], idx_flat].add(y.reshape(bsz, N_EXPERTS * cap, d))


def setup_inputs(seed: int = 0) -> dict:
    key = jax.random.key(seed)
    ks = jax.random.split(key, 24)
    f32 = jnp.float32
    D, E, F = D_MODEL, N_EXPERTS, EXPERT_FF
    nrm = lambda k, shape, s: jax.random.normal(k, shape, f32) * s
    gain = lambda k, shape: 1.0 + 0.05 * jax.random.normal(k, shape, f32)
    return {
        "x": nrm(ks[0], (BATCH, SEQ, D), 1.0),
        "c": nrm(ks[1], (BATCH, D), 1.0),
        "ctx": nrm(ks[2], (BATCH, CTX_LEN, D), 1.0),
        "c_ctx": nrm(ks[3], (D,), 1.0),
        "ada_w": nrm(ks[4], (DEPTH, D, N_MOD * D), 0.5 * D ** -0.5),
        "ada_b": nrm(ks[5], (DEPTH, N_MOD * D), 0.01),
        "pre_mix_g": gain(ks[6], (DEPTH, D)),
        "post_mix_g": gain(ks[7], (DEPTH, D)),
        "pre_ffn_g": gain(ks[8], (DEPTH, D)),
        "post_ffn_g": gain(ks[9], (DEPTH, D)),
        "conv_w_in": nrm(ks[10], (N_A_LAYERS, D, 3 * D), D ** -0.5),
        "conv_k": nrm(ks[11], (N_A_LAYERS, CONV_WIDTH, D), CONV_WIDTH ** -0.5),
        "conv_w_out": nrm(ks[12], (N_A_LAYERS, D, D), D ** -0.5),
        "attn_w_qkv": nrm(ks[13], (N_B_LAYERS, D, 3 * D), D ** -0.5),
        "attn_lambda_q1": nrm(ks[14], (N_B_LAYERS, DIFF_HEAD_DIM), 0.1),
        "attn_lambda_k1": nrm(ks[15], (N_B_LAYERS, DIFF_HEAD_DIM), 0.1),
        "attn_lambda_q2": nrm(ks[16], (N_B_LAYERS, DIFF_HEAD_DIM), 0.1),
        "attn_lambda_k2": nrm(ks[17], (N_B_LAYERS, DIFF_HEAD_DIM), 0.1),
        "attn_subln_g": gain(ks[18], (N_B_LAYERS, DIFF_V_DIM)),
        "attn_w_o": nrm(ks[19], (N_B_LAYERS, D, D), D ** -0.5),
        "router_w": nrm(ks[20], (DEPTH, D, E), D ** -0.5),
        "moe_w_gate": nrm(ks[21], (DEPTH, E, D, F), D ** -0.5),
        "moe_w_up": nrm(ks[22], (DEPTH, E, D, F), D ** -0.5),
        "moe_w_down": nrm(ks[23], (DEPTH, E, F, D), F ** -0.5),
    }


def reference(x, c, ctx, c_ctx, ada_w, ada_b, pre_mix_g, post_mix_g, pre_ffn_g, post_ffn_g,
              conv_w_in, conv_k, conv_w_out, attn_w_qkv, attn_lambda_q1, attn_lambda_k1,
              attn_lambda_q2, attn_lambda_k2, attn_subln_g, attn_w_o, router_w,
              moe_w_gate, moe_w_up, moe_w_down):
    x_lat, x_ctx = x, ctx
    ang_row, ang_col = axial_rope_angles(x.shape[1])
    for i in range(DEPTH):
        need_ctx = i < DEPTH - 1
        is_attn = (i % N_MIXERS) == 1
        j = i // N_MIXERS
        sh1, sc1, g1, sh2, sc2, g2 = ada_mod(c[:, None, :], ada_w[i], ada_b[i])
        csh1, csc1, cg1, csh2, csc2, cg2 = ada_mod(c_ctx[None, :], ada_w[i], ada_b[i])
        h_l = modulate(rmsnorm(x_lat, pre_mix_g[i]), sh1, sc1)
        m_c = None
        if need_ctx or is_attn:
            h_c = modulate(rmsnorm(x_ctx, pre_mix_g[i]), csh1, csc1)
        if not is_attn:
            m_l = short_conv_mixer(h_l, conv_w_in[j], conv_k[j], conv_w_out[j])
            if need_ctx:
                m_c = short_conv_mixer(h_c, conv_w_in[j], conv_k[j], conv_w_out[j])
        else:
            m_l, m_c = diff_attn_mixer(h_l, h_c, attn_w_qkv[j], attn_lambda_q1[j],
                                       attn_lambda_k1[j], attn_lambda_q2[j], attn_lambda_k2[j],
                                       attn_subln_g[j], attn_w_o[j], i, ang_row, ang_col,
                                       need_ctx)
        x_lat = x_lat + g1 * rmsnorm(m_l, post_mix_g[i])
        if need_ctx:
            x_ctx = x_ctx + cg1 * rmsnorm(m_c, post_mix_g[i])
        h_l = modulate(rmsnorm(x_lat, pre_ffn_g[i]), sh2, sc2)
        f_l = ec_moe(h_l, router_w[i], moe_w_gate[i], moe_w_up[i], moe_w_down[i])
        x_lat = x_lat + g2 * rmsnorm(f_l, post_ffn_g[i])
        if need_ctx:
            h_c = modulate(rmsnorm(x_ctx, pre_ffn_g[i]), csh2, csc2)
            f_c = ec_moe(h_c, router_w[i], moe_w_gate[i], moe_w_up[i], moe_w_down[i])
            x_ctx = x_ctx + cg2 * rmsnorm(f_c, post_ffn_g[i])
    return x_lat
```

```python
import functools
import math

import jax
import jax.numpy as jnp
from jax import lax
from jax.experimental import pallas as pl
from jax.experimental.pallas import tpu as pltpu

D_MODEL = 1024
N_EXPERTS = 16
EXPERT_FF = 2048
EC_FACTOR = 2
N_HEADS = 8
HEAD_DIM = 64
V_DIM = 2 * HEAD_DIM
GRID_W = 64
ROPE_BASE = 10000.0
N_MOD = 6
EPS = 1e-6

V7X_LANES = 128
V7X_SUBLANES = 8
V7X_VMEM_BYTES = 64 * 1024 * 1024
VMEM_LIMIT = V7X_VMEM_BYTES - 8 * 1024 * 1024

SEQ_TILE = 512
ATTN_Q_TILE = 256
HALO = V7X_SUBLANES

F32 = jnp.float32
BF16 = jnp.bfloat16


def _params(n_axes):
    return pltpu.CompilerParams(
        dimension_semantics=("parallel",) * n_axes, vmem_limit_bytes=VMEM_LIMIT)


def _rms(x, g):
    return x * lax.rsqrt(jnp.mean(x * x, axis=-1, keepdims=True) + EPS) * g


def _silu(x):
    return x * (1.0 / (1.0 + jnp.exp(-x)))


def _bdot(a, b):
    return jnp.dot(a, b, preferred_element_type=F32)


def _ada_kernel(cond_ref, w_ref, b_ref, o_ref):
    s = _silu(cond_ref[...])
    o_ref[0] = jnp.dot(s, w_ref[0], precision=lax.Precision.HIGHEST,
                       preferred_element_type=F32) + b_ref[0]


def _ada(cond, ada_w, ada_b):
    depth, d, nd = ada_w.shape
    rows = cond.shape[0]
    tn = nd // 4
    return pl.pallas_call(
        _ada_kernel,
        grid=(depth, nd // tn),
        in_specs=[
            pl.BlockSpec((rows, d), lambda i, j: (0, 0)),
            pl.BlockSpec((1, d, tn), lambda i, j: (i, 0, j)),
            pl.BlockSpec((1, 1, tn), lambda i, j: (i, 0, j)),
        ],
        out_specs=pl.BlockSpec((1, rows, tn), lambda i, j: (i, 0, j)),
        out_shape=jax.ShapeDtypeStruct((depth, rows, nd), F32),
        compiler_params=_params(2),
        name="ada",
    )(cond, ada_w, ada_b.reshape(depth, 1, nd))


def _mixer_tail(x, m, mods, gpost_ref, gffn_ref, wr_ref, xo_ref, h2_ref, aff_ref):
    g1, sh2, sc2 = mods[2:3], mods[3:4], mods[4:5]
    xn = x + g1 * _rms(m, gpost_ref[...])
    xo_ref[0] = xn
    h2 = _rms(xn, gffn_ref[...]) * (1.0 + sc2) + sh2
    h2_ref[0] = h2.astype(BF16)
    logits = lax.dot_general(wr_ref[...], h2, (((1,), (1,)), ((), ())),
                             precision=lax.Precision.HIGHEST, preferred_element_type=F32)
    ex = jnp.exp(logits - jnp.max(logits, axis=0, keepdims=True))
    aff_ref[0] = ex / jnp.sum(ex, axis=0, keepdims=True)


def _conv_kernel(x_ref, xp_ref, xn_ref, mods_ref, gpre_ref, gpost_ref, gffn_ref,
                 win_ref, ck_ref, wout_ref, wr_ref, xo_ref, h2_ref, aff_ref, *, ts):
    d = D_MODEL
    t = pl.program_id(1)
    last = pl.num_programs(1) - 1
    mods = mods_ref[0]
    sh1, sc1 = mods[0:1], mods[1:2]
    x = x_ref[0]
    rows = ts + 2 * HALO
    xe = jnp.concatenate([xp_ref[0], x, xn_ref[0]], axis=0)
    h = _rms(xe, gpre_ref[...]) * (1.0 + sc1) + sh1
    proj = _bdot(h.astype(BF16), win_ref[...])
    u = proj[:, d:2 * d] * proj[:, 2 * d:]
    r = lax.broadcasted_iota(jnp.int32, (rows, 1), 0)
    inside = jnp.logical_and(jnp.logical_or(r >= HALO, t > 0),
                             jnp.logical_or(r < ts + HALO, t < last))
    u = jnp.where(inside, u, 0.0)
    ck = ck_ref[...]
    u_prev = pltpu.roll(u, 1, axis=0)[HALO:HALO + ts]
    u_next = pltpu.roll(u, rows - 1, axis=0)[HALO:HALO + ts]
    conv = ck[0:1] * u_prev + ck[1:2] * u[HALO:HALO + ts] + ck[2:3] * u_next
    z = proj[HALO:HALO + ts, :d] * conv
    m = _bdot(z.astype(BF16), wout_ref[...])
    _mixer_tail(x, m, mods, gpost_ref, gffn_ref, wr_ref, xo_ref, h2_ref, aff_ref)


def _const_spec(shape):
    return pl.BlockSpec(shape, lambda *_: (0,) * len(shape))


def _mods_spec(mods):
    if mods.shape[0] == 1:
        return pl.BlockSpec((1, N_MOD, D_MODEL), lambda b, t: (0, 0, 0))
    return pl.BlockSpec((1, N_MOD, D_MODEL), lambda b, t: (b, 0, 0))


def _tail_out(bsz, n, ts):
    d, e = D_MODEL, N_EXPERTS
    specs = [
        pl.BlockSpec((1, ts, d), lambda b, t: (b, t, 0)),
        pl.BlockSpec((1, ts, d), lambda b, t: (b, t, 0)),
        pl.BlockSpec((1, e, ts), lambda b, t: (b, 0, t)),
    ]
    shapes = [
        jax.ShapeDtypeStruct((bsz, n, d), F32),
        jax.ShapeDtypeStruct((bsz, n, d), BF16),
        jax.ShapeDtypeStruct((bsz, e, n), F32),
    ]
    return specs, shapes


def _conv_mixer(x, mods, gpre, gpost, gffn, w_in, ck, w_out, wr_t):
    bsz, n, d = x.shape
    ts = min(SEQ_TILE, n)
    nblk = n // HALO
    per = ts // HALO
    out_specs, out_shapes = _tail_out(bsz, n, ts)
    return pl.pallas_call(
        functools.partial(_conv_kernel, ts=ts),
        grid=(bsz, n // ts),
        in_specs=[
            pl.BlockSpec((1, ts, d), lambda b, t: (b, t, 0)),
            pl.BlockSpec((1, HALO, d), lambda b, t: (b, jnp.maximum(t * per - 1, 0), 0)),
            pl.BlockSpec((1, HALO, d), lambda b, t: (b, jnp.minimum((t + 1) * per, nblk - 1), 0)),
            _mods_spec(mods),
            _const_spec((1, d)), _const_spec((1, d)), _const_spec((1, d)),
            _const_spec((d, 3 * d)),
            _const_spec((3, d)),
            _const_spec((d, d)),
            _const_spec((N_EXPERTS, d)),
        ],
        out_specs=out_specs,
        out_shape=out_shapes,
        compiler_params=_params(2),
        name="conv_mixer",
    )(x, x, x, mods, gpre, gpost, gffn, w_in, ck, w_out, wr_t)


def _excl_cumsum(mask_f):
    rows, n = mask_f.shape
    c = V7X_LANES
    ri = lax.broadcasted_iota(jnp.int32, (c, c), 0)
    ci = lax.broadcasted_iota(jnp.int32, (c, c), 1)
    tri = jnp.where(ri < ci, 1.0, 0.0).astype(BF16)
    ones = jnp.ones((c, c), BF16)
    carry = jnp.zeros((rows, c), F32)
    outs = []
    for j in range(n // c):
        ch = mask_f[:, j * c:(j + 1) * c].astype(BF16)
        outs.append(_bdot(ch, tri) + carry)
        carry = carry + _bdot(ch, ones)
    return jnp.concatenate(outs, axis=1)


def _topk_kernel(aff_ref, slot_ref, *, cap):
    a = aff_ref[0]
    bits = lax.bitcast_convert_type(a, jnp.int32)
    rows = a.shape[0]
    thr = jnp.zeros((rows, 1), jnp.int32)
    capf = float(cap)
    for bit in range(30, -1, -1):
        cand = thr | (1 << bit)
        cnt = jnp.sum(jnp.where(bits >= cand, 1.0, 0.0), axis=1, keepdims=True)
        thr = jnp.where(cnt >= capf, cand, thr)
    gt = jnp.where(bits > thr, 1.0, 0.0)
    eq = jnp.where(bits == thr, 1.0, 0.0)
    need = capf - jnp.sum(gt, axis=1, keepdims=True)
    sel = gt + eq * jnp.where(_excl_cumsum(eq) < need, 1.0, 0.0)
    slot = _excl_cumsum(sel)
    slot_ref[0] = jnp.where(sel > 0.0, slot, -1.0).astype(jnp.int32)


def _topk(aff_t, cap):
    bsz, e, n = aff_t.shape
    return pl.pallas_call(
        functools.partial(_topk_kernel, cap=cap),
        grid=(bsz,),
        in_specs=[pl.BlockSpec((1, e, n), lambda b: (b, 0, 0))],
        out_specs=pl.BlockSpec((1, e, n), lambda b: (b, 0, 0)),
        out_shape=jax.ShapeDtypeStruct((bsz, e, n), jnp.int32),
        compiler_params=_params(1),
        name="topk",
    )(aff_t)


def _expert_kernel(slot_ref, aff_ref, h_ref, wg_ref, wu_ref, wd_ref, ys_ref, xs_ref,
                   *, cap, bb):
    n = h_ref.shape[1]
    srow = lax.broadcasted_iota(jnp.int32, (cap, n), 0)
    vals = []
    for i in range(bb):
        hit = slot_ref[i, 0] == srow
        vals.append(jnp.sum(jnp.where(hit, aff_ref[i, 0], 0.0), axis=1, keepdims=True))
        onehot = jnp.where(hit, 1.0, 0.0).astype(BF16)
        xs_ref[i * cap:(i + 1) * cap, :] = _bdot(onehot, h_ref[i]).astype(BF16)
    xs = xs_ref[...]
    g = _bdot(xs, wg_ref[0])
    u = _bdot(xs, wu_ref[0])
    y = _bdot((_silu(g) * u).astype(BF16), wd_ref[0])
    for i in range(bb):
        ys_ref[i, 0] = (y[i * cap:(i + 1) * cap] * vals[i]).astype(BF16)


def _experts(slot_t, aff_t, h2, wg, wu, wd, cap, bb):
    bsz, e, n = slot_t.shape
    d, f = D_MODEL, EXPERT_FF
    idx4 = lambda ei, bi: (bi, ei, 0, 0)
    return pl.pallas_call(
        functools.partial(_expert_kernel, cap=cap, bb=bb),
        grid=(e, bsz // bb),
        in_specs=[
            pl.BlockSpec((bb, 1, 1, n), idx4),
            pl.BlockSpec((bb, 1, 1, n), idx4),
            pl.BlockSpec((bb, n, d), lambda ei, bi: (bi, 0, 0)),
            pl.BlockSpec((1, d, f), lambda ei, bi: (ei, 0, 0)),
            pl.BlockSpec((1, d, f), lambda ei, bi: (ei, 0, 0)),
            pl.BlockSpec((1, f, d), lambda ei, bi: (ei, 0, 0)),
        ],
        out_specs=pl.BlockSpec((bb, 1, cap, d), idx4),
        out_shape=jax.ShapeDtypeStruct((bsz, e, cap, d), BF16),
        scratch_shapes=[pltpu.VMEM((bb * cap, d), BF16)],
        compiler_params=_params(2),
        name="experts",
    )(slot_t.reshape(bsz, e, 1, n), aff_t.reshape(bsz, e, 1, n), h2, wg, wu, wd)


def _combine_kernel(slot_ref, ys_ref, x_ref, mods_ref, gpost_ref, xo_ref, *, cap):
    e = N_EXPERTS
    sl = slot_ref[0]
    tn = sl.shape[0]
    if cap % V7X_LANES == 0:
        lane = lax.broadcasted_iota(jnp.int32, (1, cap), 1)
        pt = jnp.concatenate(
            [jnp.where(sl[:, j:j + 1] == lane, 1.0, 0.0).astype(BF16) for j in range(e)],
            axis=1)
    else:
        lane = lax.broadcasted_iota(jnp.int32, (1, e * cap), 1)
        acc = jnp.zeros((tn, e * cap), F32)
        for j in range(e):
            col = sl[:, j:j + 1]
            acc = acc + jnp.where(jnp.where(col >= 0, col + j * cap, -1) == lane, 1.0, 0.0)
        pt = acc.astype(BF16)
    f = _bdot(pt, ys_ref[0])
    g2 = mods_ref[0][5:6]
    xo_ref[0] = x_ref[0] + g2 * _rms(f, gpost_ref[...])


def _combine(slot, ys, x, mods, gpost, cap):
    bsz, n, d = x.shape
    e = N_EXPERTS
    tn = min(SEQ_TILE, n)
    return pl.pallas_call(
        functools.partial(_combine_kernel, cap=cap),
        grid=(bsz, n // tn),
        in_specs=[
            pl.BlockSpec((1, tn, e), lambda b, t: (b, t, 0)),
            pl.BlockSpec((1, e * cap, d), lambda b, t: (b, 0, 0)),
            pl.BlockSpec((1, tn, d), lambda b, t: (b, t, 0)),
            _mods_spec(mods),
            _const_spec((1, d)),
        ],
        out_specs=pl.BlockSpec((1, tn, d), lambda b, t: (b, t, 0)),
        out_shape=jax.ShapeDtypeStruct((bsz, n, d), F32),
        compiler_params=_params(2),
        name="combine",
    )(slot, ys.reshape(bsz, e * cap, d), x, mods, gpost)


def _ec_moe(x, h2, aff_t, mods, gpost, wg, wu, wd, bb):
    n = x.shape[1]
    cap = EC_FACTOR * n // N_EXPERTS
    slot_t = _topk(aff_t, cap)
    ys = _experts(slot_t, aff_t, h2, wg, wu, wd, cap, bb)
    slot = jnp.swapaxes(slot_t, 1, 2)
    return _combine(slot, ys, x, mods, gpost, cap)


def _qkv_kernel(x_ref, mods_ref, gpre_ref, w_ref, cos_ref, sin_ref, *out_refs, with_q, rope):
    d = D_MODEL
    mods = mods_ref[0]
    sh1, sc1 = mods[0:1], mods[1:2]
    h = _rms(x_ref[0], gpre_ref[...]) * (1.0 + sc1) + sh1
    proj = _bdot(h.astype(BF16), w_ref[...])
    c = V7X_LANES
    lane = lax.broadcasted_iota(jnp.int32, (1, c), 1)
    first_half = (lane % (HEAD_DIM // 2)) < (HEAD_DIM // 4)
    cos = cos_ref[...]
    sin = sin_ref[...]
    n_rot = 2 if with_q else 1
    for a in range(n_rot):
        scale = HEAD_DIM ** -0.5 if (with_q and a == 0) else 1.0
        for j in range(d // c):
            ch = proj[:, a * d + j * c: a * d + (j + 1) * c]
            if rope:
                partner = jnp.where(first_half,
                                    pltpu.roll(ch, c - HEAD_DIM // 4, axis=1),
                                    pltpu.roll(ch, HEAD_DIM // 4, axis=1))
                ch = ch * cos + partner * sin
            if scale != 1.0:
                ch = ch * scale
            out_refs[a][0, :, j * c:(j + 1) * c] = ch.astype(BF16)
    out_refs[n_rot][0] = proj[:, n_rot * d:].astype(BF16)


def _qkv(x, mods, gpre, w, cos, sin, with_q, rope):
    bsz, n, d = x.shape
    ts = min(SEQ_TILE, n)
    n_out = 3 if with_q else 2
    tok = pl.BlockSpec((1, ts, d), lambda b, t: (b, t, 0))
    tab = pl.BlockSpec((ts, V7X_LANES), lambda b, t: (t, 0))
    return pl.pallas_call(
        functools.partial(_qkv_kernel, with_q=with_q, rope=rope),
        grid=(bsz, n // ts),
        in_specs=[tok, _mods_spec(mods), _const_spec((1, d)), _const_spec((d, n_out * d)),
                  tab, tab],
        out_specs=[tok] * n_out,
        out_shape=[jax.ShapeDtypeStruct((bsz, n, d), BF16)] * n_out,
        compiler_params=_params(2),
        name="qkv",
    )(x, mods, gpre, w, cos, sin)


def _attn_kernel(q_ref, kl_ref, vl_ref, kc_ref, vc_ref, lam_ref, g_ref, o_ref, *, lam_init):
    lm = lam_ref[...]
    lam = (jnp.exp(jnp.sum(lm[0:1] * lm[1:2], axis=1, keepdims=True))
           - jnp.exp(jnp.sum(lm[2:3] * lm[3:4], axis=1, keepdims=True)) + lam_init)
    q = q_ref[0]
    kl = kl_ref[0]
    kc = kc_ref[0]
    lane = lax.broadcasted_iota(jnp.int32, (1, V_DIM), 1)
    nt = (((1,), (1,)), ((), ()))
    probs = []
    for comp in range(2):
        in_comp = (lane >= comp * HEAD_DIM) & (lane < (comp + 1) * HEAD_DIM)
        qc = jnp.where(in_comp, q, jnp.zeros_like(q))
        s_l = lax.dot_general(qc, kl, nt, preferred_element_type=F32)
        s_c = lax.dot_general(qc, kc, nt, preferred_element_type=F32)
        m = jnp.maximum(jnp.max(s_l, axis=1, keepdims=True), jnp.max(s_c, axis=1, keepdims=True))
        e_l = jnp.exp(s_l - m)
        e_c = jnp.exp(s_c - m)
        r = 1.0 / (jnp.sum(e_l, axis=1, keepdims=True) + jnp.sum(e_c, axis=1, keepdims=True))
        probs.append((e_l * r, e_c * r))
    a_l = (probs[0][0] - lam * probs[1][0]).astype(BF16)
    a_c = (probs[0][1] - lam * probs[1][1]).astype(BF16)
    o = _bdot(a_l, vl_ref[0]) + _bdot(a_c, vc_ref[0])
    o_ref[0] = (_rms(o, g_ref[...]) * (1.0 - lam_init)).astype(BF16)


def _attention(q, k_l, v_l, k_c, v_c, lam_rows, subln_g, lam_init):
    bsz, n, d = q.shape
    nc = k_c.shape[1]
    tq = ATTN_Q_TILE
    hw = V_DIM
    return pl.pallas_call(
        functools.partial(_attn_kernel, lam_init=lam_init),
        grid=(bsz, N_HEADS, n // tq),
        in_specs=[
            pl.BlockSpec((1, tq, hw), lambda b, h, t: (b, t, h)),
            pl.BlockSpec((1, n, hw), lambda b, h, t: (b, 0, h)),
            pl.BlockSpec((1, n, hw), lambda b, h, t: (b, 0, h)),
            pl.BlockSpec((1, nc, hw), lambda b, h, t: (b, 0, h)),
            pl.BlockSpec((1, nc, hw), lambda b, h, t: (b, 0, h)),
            _const_spec((4, HEAD_DIM)),
            _const_spec((1, hw)),
        ],
        out_specs=pl.BlockSpec((1, tq, hw), lambda b, h, t: (b, t, h)),
        out_shape=jax.ShapeDtypeStruct((bsz, n, d), BF16),
        compiler_params=_params(3),
        name="diff_attn",
    )(q, k_l, v_l, k_c, v_c, lam_rows, subln_g)


def _oproj_kernel(o_ref, x_ref, mods_ref, gpost_ref, gffn_ref, wo_ref, wr_ref,
                  xo_ref, h2_ref, aff_ref):
    m = _bdot(o_ref[0], wo_ref[...])
    _mixer_tail(x_ref[0], m, mods_ref[0], gpost_ref, gffn_ref, wr_ref, xo_ref, h2_ref, aff_ref)


def _oproj(o, x, mods, gpost, gffn, w_o, wr_t):
    bsz, n, d = x.shape
    ts = min(SEQ_TILE, n)
    tok = pl.BlockSpec((1, ts, d), lambda b, t: (b, t, 0))
    out_specs, out_shapes = _tail_out(bsz, n, ts)
    return pl.pallas_call(
        _oproj_kernel,
        grid=(bsz, n // ts),
        in_specs=[tok, tok, _mods_spec(mods), _const_spec((1, d)), _const_spec((1, d)),
                  _const_spec((d, d)), _const_spec((N_EXPERTS, d))],
        out_specs=out_specs,
        out_shape=out_shapes,
        compiler_params=_params(2),
        name="oproj",
    )(o, x, mods, gpost, gffn, w_o, wr_t)


def _rope_tables(n):
    pos = jnp.arange(n)
    row = (pos // GRID_W).astype(F32)
    col = (pos % GRID_W).astype(F32)
    n_freq = HEAD_DIM // 4
    inv = 1.0 / (ROPE_BASE ** (jnp.arange(n_freq, dtype=F32) / n_freq))
    ar = row[:, None] * inv
    ac = col[:, None] * inv
    cos64 = jnp.concatenate([jnp.cos(ar), jnp.cos(ar), jnp.cos(ac), jnp.cos(ac)], axis=1)
    sin64 = jnp.concatenate([-jnp.sin(ar), jnp.sin(ar), -jnp.sin(ac), jnp.sin(ac)], axis=1)
    reps = V7X_LANES // HEAD_DIM
    return jnp.tile(cos64, (1, reps)), jnp.tile(sin64, (1, reps))


def kernel(x, c, ctx, c_ctx, ada_w, ada_b, pre_mix_g, post_mix_g, pre_ffn_g, post_ffn_g,
           conv_w_in, conv_k, conv_w_out, attn_w_qkv, attn_lambda_q1, attn_lambda_k1,
           attn_lambda_q2, attn_lambda_k2, attn_subln_g, attn_w_o, router_w,
           moe_w_gate, moe_w_up, moe_w_down):
    bsz, n, d = x.shape
    depth = ada_w.shape[0]
    assert (depth, d) == (2, D_MODEL) and n % SEQ_TILE == 0 and n % GRID_W == 0

    pad = (-(bsz + 1)) % V7X_SUBLANES
    cond = jnp.concatenate([c, c_ctx[None, :], jnp.zeros((pad, d), F32)], axis=0)
    ada = _ada(cond, ada_w, ada_b)
    mods_l = [ada[i, :bsz].reshape(bsz, N_MOD, d) for i in range(depth)]
    mods_c = [ada[i, bsz:bsz + 1].reshape(1, N_MOD, d) for i in range(depth)]
    row = lambda g, i: g[i].reshape(1, d)
    wr_t = [jnp.swapaxes(router_w[i], 0, 1) for i in range(depth)]
    wg, wu, wd = (w.astype(BF16) for w in (moe_w_gate, moe_w_up, moe_w_down))

    conv_args = (row(pre_mix_g, 0), row(post_mix_g, 0), row(pre_ffn_g, 0),
                 conv_w_in[0].astype(BF16), conv_k[0], conv_w_out[0].astype(BF16), wr_t[0])
    x_l, h2_l, aff_l = _conv_mixer(x, mods_l[0], *conv_args)
    x_c, h2_c, aff_c = _conv_mixer(ctx, mods_c[0], *conv_args)
    gp0 = row(post_ffn_g, 0)
    x_l = _ec_moe(x_l, h2_l, aff_l, mods_l[0], gp0, wg[0], wu[0], wd[0], bb=1)
    x_c = _ec_moe(x_c, h2_c, aff_c, mods_c[0], gp0, wg[0], wu[0], wd[0], bb=bsz)

    lam_init = 0.8 - 0.6 * math.exp(-0.3 * 1)
    cos, sin = _rope_tables(n)
    w_qkv = attn_w_qkv[0].astype(BF16)
    gpre1 = row(pre_mix_g, 1)
    q, k_l, v_l = _qkv(x_l, mods_l[1], gpre1, w_qkv, cos, sin, with_q=True, rope=True)
    nc = ctx.shape[1]
    k_c, v_c = _qkv(x_c, mods_c[1], gpre1, w_qkv[:, d:], cos[:nc], sin[:nc],
                    with_q=False, rope=False)
    lam_rows = jnp.stack([attn_lambda_q1[0], attn_lambda_k1[0],
                          attn_lambda_q2[0], attn_lambda_k2[0]], axis=0)
    o = _attention(q, k_l, v_l, k_c, v_c, lam_rows, attn_subln_g[0].reshape(1, V_DIM), lam_init)
    x_l, h2_l, aff_l = _oproj(o, x_l, mods_l[1], row(post_mix_g, 1), row(pre_ffn_g, 1),
                              attn_w_o[0].astype(BF16), wr_t[1])
    return _ec_moe(x_l, h2_l, aff_l, mods_l[1], row(post_ffn_g, 1), wg[1], wu[1], wd[1], bb=1)
```

```python
import functools
import math

import jax
import jax.numpy as jnp
from jax import lax
from jax.experimental import pallas as pl
from jax.experimental.pallas import tpu as pltpu

D_MODEL = 1024
N_EXPERTS = 16
EXPERT_FF = 2048
EC_FACTOR = 2
N_HEADS = 8
HEAD_DIM = 64
V_DIM = 2 * HEAD_DIM
GRID_W = 64
ROPE_BASE = 10000.0
N_MOD = 6
EPS = 1e-6

V7X_LANES = 128
V7X_SUBLANES = 8
V7X_VMEM_BYTES = 64 * 1024 * 1024
VMEM_LIMIT = V7X_VMEM_BYTES - 8 * 1024 * 1024

SEQ_TILE = 512
ATTN_Q_TILE = 1024
ATTN_SUB_Q = 256
ATTN_K_CHUNK = 512
HALO = V7X_SUBLANES

F32 = jnp.float32
BF16 = jnp.bfloat16


def _params(n_axes):
    return pltpu.CompilerParams(
        dimension_semantics=("parallel",) * n_axes, vmem_limit_bytes=VMEM_LIMIT)


def _rms(x, g):
    return x * lax.rsqrt(jnp.mean(x * x, axis=-1, keepdims=True) + EPS) * g


def _silu(x):
    return x * (1.0 / (1.0 + jnp.exp(-x)))


def _bdot(a, b):
    return jnp.dot(a, b, preferred_element_type=F32)


def _ada_kernel(cond_ref, w_ref, b_ref, o_ref):
    s = _silu(cond_ref[...])
    o_ref[0] = jnp.dot(s, w_ref[0], precision=lax.Precision.HIGHEST,
                       preferred_element_type=F32) + b_ref[0]


def _ada(cond, ada_w, ada_b):
    depth, d, nd = ada_w.shape
    rows = cond.shape[0]
    tn = nd // 4
    return pl.pallas_call(
        _ada_kernel,
        grid=(depth, nd // tn),
        in_specs=[
            pl.BlockSpec((rows, d), lambda i, j: (0, 0)),
            pl.BlockSpec((1, d, tn), lambda i, j: (i, 0, j)),
            pl.BlockSpec((1, 1, tn), lambda i, j: (i, 0, j)),
        ],
        out_specs=pl.BlockSpec((1, rows, tn), lambda i, j: (i, 0, j)),
        out_shape=jax.ShapeDtypeStruct((depth, rows, nd), F32),
        compiler_params=_params(2),
        name="ada",
    )(cond, ada_w, ada_b.reshape(depth, 1, nd))


def _mixer_tail(x, m, mods, gpost_ref, gffn_ref, wr_ref, xo_ref, h2_ref, aff_ref):
    g1, sh2, sc2 = mods[2:3], mods[3:4], mods[4:5]
    xn = x + g1 * _rms(m, gpost_ref[...])
    xo_ref[0] = xn
    h2 = _rms(xn, gffn_ref[...]) * (1.0 + sc2) + sh2
    h2_ref[0] = h2.astype(BF16)
    logits = lax.dot_general(wr_ref[...], h2, (((1,), (1,)), ((), ())),
                             precision=lax.Precision.HIGHEST, preferred_element_type=F32)
    ex = jnp.exp(logits - jnp.max(logits, axis=0, keepdims=True))
    aff_ref[0] = ex / jnp.sum(ex, axis=0, keepdims=True)


def _conv_kernel(x_ref, xp_ref, xn_ref, mods_ref, gpre_ref, gpost_ref, gffn_ref,
                 win_ref, ck_ref, wout_ref, wr_ref, xo_ref, h2_ref, aff_ref, *, ts):
    d = D_MODEL
    t = pl.program_id(1)
    last = pl.num_programs(1) - 1
    mods = mods_ref[0]
    sh1, sc1 = mods[0:1], mods[1:2]
    x = x_ref[0]
    rows = ts + 2 * HALO
    xe = jnp.concatenate([xp_ref[0], x, xn_ref[0]], axis=0)
    h = _rms(xe, gpre_ref[...]) * (1.0 + sc1) + sh1
    proj = _bdot(h.astype(BF16), win_ref[...])
    u = proj[:, d:2 * d] * proj[:, 2 * d:]
    r = lax.broadcasted_iota(jnp.int32, (rows, 1), 0)
    inside = jnp.logical_and(jnp.logical_or(r >= HALO, t > 0),
                             jnp.logical_or(r < ts + HALO, t < last))
    u = jnp.where(inside, u, 0.0)
    ck = ck_ref[...]
    u_prev = pltpu.roll(u, 1, axis=0)[HALO:HALO + ts]
    u_next = pltpu.roll(u, rows - 1, axis=0)[HALO:HALO + ts]
    conv = ck[0:1] * u_prev + ck[1:2] * u[HALO:HALO + ts] + ck[2:3] * u_next
    z = proj[HALO:HALO + ts, :d] * conv
    m = _bdot(z.astype(BF16), wout_ref[...])
    _mixer_tail(x, m, mods, gpost_ref, gffn_ref, wr_ref, xo_ref, h2_ref, aff_ref)


def _const_spec(shape):
    return pl.BlockSpec(shape, lambda *_: (0,) * len(shape))


def _mods_spec(mods):
    if mods.shape[0] == 1:
        return pl.BlockSpec((1, N_MOD, D_MODEL), lambda b, t: (0, 0, 0))
    return pl.BlockSpec((1, N_MOD, D_MODEL), lambda b, t: (b, 0, 0))


def _tail_out(bsz, n, ts):
    d, e = D_MODEL, N_EXPERTS
    specs = [
        pl.BlockSpec((1, ts, d), lambda b, t: (b, t, 0)),
        pl.BlockSpec((1, ts, d), lambda b, t: (b, t, 0)),
        pl.BlockSpec((1, e, ts), lambda b, t: (b, 0, t)),
    ]
    shapes = [
        jax.ShapeDtypeStruct((bsz, n, d), F32),
        jax.ShapeDtypeStruct((bsz, n, d), BF16),
        jax.ShapeDtypeStruct((bsz, e, n), F32),
    ]
    return specs, shapes


def _conv_mixer(x, mods, gpre, gpost, gffn, w_in, ck, w_out, wr_t):
    bsz, n, d = x.shape
    ts = min(SEQ_TILE, n)
    nblk = n // HALO
    per = ts // HALO
    out_specs, out_shapes = _tail_out(bsz, n, ts)
    return pl.pallas_call(
        functools.partial(_conv_kernel, ts=ts),
        grid=(bsz, n // ts),
        in_specs=[
            pl.BlockSpec((1, ts, d), lambda b, t: (b, t, 0)),
            pl.BlockSpec((1, HALO, d), lambda b, t: (b, jnp.maximum(t * per - 1, 0), 0)),
            pl.BlockSpec((1, HALO, d), lambda b, t: (b, jnp.minimum((t + 1) * per, nblk - 1), 0)),
            _mods_spec(mods),
            _const_spec((1, d)), _const_spec((1, d)), _const_spec((1, d)),
            _const_spec((d, 3 * d)),
            _const_spec((3, d)),
            _const_spec((d, d)),
            _const_spec((N_EXPERTS, d)),
        ],
        out_specs=out_specs,
        out_shape=out_shapes,
        compiler_params=_params(2),
        name="conv_mixer",
    )(x, x, x, mods, gpre, gpost, gffn, w_in, ck, w_out, wr_t)


def _excl_cumsum(mask_f):
    rows, n = mask_f.shape
    c = V7X_LANES
    ri = lax.broadcasted_iota(jnp.int32, (c, c), 0)
    ci = lax.broadcasted_iota(jnp.int32, (c, c), 1)
    tri = jnp.where(ri < ci, 1.0, 0.0).astype(BF16)
    ones = jnp.ones((c, c), BF16)
    carry = jnp.zeros((rows, c), F32)
    outs = []
    for j in range(n // c):
        ch = mask_f[:, j * c:(j + 1) * c].astype(BF16)
        outs.append(_bdot(ch, tri) + carry)
        carry = carry + _bdot(ch, ones)
    return jnp.concatenate(outs, axis=1)


def _topk_kernel(aff_ref, slot_ref, *, cap):
    a = aff_ref[0]
    rows = a.shape[0]
    thr = jnp.zeros((rows, 1), jnp.int32)
    capf = float(cap)
    for bit in range(30, -1, -1):
        cand = thr | (1 << bit)
        cand_f = lax.bitcast_convert_type(cand, F32)
        cnt = jnp.sum(jnp.where(a >= cand_f, 1.0, 0.0), axis=1, keepdims=True)
        thr = jnp.where(cnt >= capf, cand, thr)
    thr_f = lax.bitcast_convert_type(thr, F32)
    gt = jnp.where(a > thr_f, 1.0, 0.0)
    eq = jnp.where(a == thr_f, 1.0, 0.0)
    need = capf - jnp.sum(gt, axis=1, keepdims=True)
    sel = gt + eq * jnp.where(_excl_cumsum(eq) < need, 1.0, 0.0)
    slot = _excl_cumsum(sel)
    slot_ref[0] = jnp.where(sel > 0.0, slot, -1.0).astype(jnp.int32)


def _topk(aff_t, cap):
    bsz, e, n = aff_t.shape
    return pl.pallas_call(
        functools.partial(_topk_kernel, cap=cap),
        grid=(bsz,),
        in_specs=[pl.BlockSpec((1, e, n), lambda b: (b, 0, 0))],
        out_specs=pl.BlockSpec((1, e, n), lambda b: (b, 0, 0)),
        out_shape=jax.ShapeDtypeStruct((bsz, e, n), jnp.int32),
        compiler_params=_params(1),
        name="topk",
    )(aff_t)


def _expert_kernel(slot_ref, aff_ref, h_ref, wg_ref, wu_ref, wd_ref, ys_ref, xs_ref,
                   *, cap, bb):
    n = h_ref.shape[1]
    srow = lax.broadcasted_iota(jnp.int32, (cap, n), 0)
    vals = []
    for i in range(bb):
        hit = slot_ref[i, 0] == srow
        vals.append(jnp.sum(jnp.where(hit, aff_ref[i, 0], 0.0), axis=1, keepdims=True))
        onehot = jnp.where(hit, 1.0, 0.0).astype(BF16)
        xs_ref[i * cap:(i + 1) * cap, :] = _bdot(onehot, h_ref[i]).astype(BF16)
    xs = xs_ref[...]
    g = _bdot(xs, wg_ref[0, 0])
    u = _bdot(xs, wu_ref[0, 0])
    y = _bdot((_silu(g) * u).astype(BF16), wd_ref[0, 0])
    for i in range(bb):
        ys_ref[i, 0] = (y[i * cap:(i + 1) * cap] * vals[i]).astype(BF16)


def _experts(slot_t, aff_t, h2, wg, wu, wd, layer, cap, bb):
    bsz, e, n = slot_t.shape
    d, f = D_MODEL, EXPERT_FF
    idx4 = lambda ei, bi: (bi, ei, 0, 0)
    widx = lambda ei, bi: (layer, ei, 0, 0)
    return pl.pallas_call(
        functools.partial(_expert_kernel, cap=cap, bb=bb),
        grid=(e, bsz // bb),
        in_specs=[
            pl.BlockSpec((bb, 1, 1, n), idx4),
            pl.BlockSpec((bb, 1, 1, n), idx4),
            pl.BlockSpec((bb, n, d), lambda ei, bi: (bi, 0, 0)),
            pl.BlockSpec((1, 1, d, f), widx),
            pl.BlockSpec((1, 1, d, f), widx),
            pl.BlockSpec((1, 1, f, d), widx),
        ],
        out_specs=pl.BlockSpec((bb, 1, cap, d), idx4),
        out_shape=jax.ShapeDtypeStruct((bsz, e, cap, d), BF16),
        scratch_shapes=[pltpu.VMEM((bb * cap, d), BF16)],
        compiler_params=_params(2),
        name="experts",
    )(slot_t.reshape(bsz, e, 1, n), aff_t.reshape(bsz, e, 1, n), h2, wg, wu, wd)


def _combine_kernel(slot_ref, ys_ref, x_ref, mods_ref, gpost_ref, xo_ref, *, cap):
    e = N_EXPERTS
    sl = slot_ref[0]
    tn = sl.shape[0]
    if cap % V7X_LANES == 0:
        lane = lax.broadcasted_iota(jnp.int32, (1, cap), 1)
        pt = jnp.concatenate(
            [jnp.where(sl[:, j:j + 1] == lane, 1.0, 0.0).astype(BF16) for j in range(e)],
            axis=1)
    else:
        lane = lax.broadcasted_iota(jnp.int32, (1, e * cap), 1)
        acc = jnp.zeros((tn, e * cap), F32)
        for j in range(e):
            col = sl[:, j:j + 1]
            acc = acc + jnp.where(jnp.where(col >= 0, col + j * cap, -1) == lane, 1.0, 0.0)
        pt = acc.astype(BF16)
    f = _bdot(pt, ys_ref[0])
    g2 = mods_ref[0][5:6]
    xo_ref[0] = x_ref[0] + g2 * _rms(f, gpost_ref[...])


def _combine(slot, ys, x, mods, gpost, cap):
    bsz, n, d = x.shape
    e = N_EXPERTS
    tn = min(SEQ_TILE, n)
    return pl.pallas_call(
        functools.partial(_combine_kernel, cap=cap),
        grid=(bsz, n // tn),
        in_specs=[
            pl.BlockSpec((1, tn, e), lambda b, t: (b, t, 0)),
            pl.BlockSpec((1, e * cap, d), lambda b, t: (b, 0, 0)),
            pl.BlockSpec((1, tn, d), lambda b, t: (b, t, 0)),
            _mods_spec(mods),
            _const_spec((1, d)),
        ],
        out_specs=pl.BlockSpec((1, tn, d), lambda b, t: (b, t, 0)),
        out_shape=jax.ShapeDtypeStruct((bsz, n, d), F32),
        compiler_params=_params(2),
        name="combine",
    )(slot, ys.reshape(bsz, e * cap, d), x, mods, gpost)


def _ec_moe(x, h2, aff_t, mods, gpost, wg, wu, wd, layer, bb):
    n = x.shape[1]
    cap = EC_FACTOR * n // N_EXPERTS
    slot_t = _topk(aff_t, cap)
    ys = _experts(slot_t, aff_t, h2, wg, wu, wd, layer, cap, bb)
    slot = jnp.swapaxes(slot_t, 1, 2)
    return _combine(slot, ys, x, mods, gpost, cap)


NT_DIMS = (((1,), (1,)), ((), ()))


def _qkv_kernel(x_ref, mods_ref, gpre_ref, w_ref, wvt_ref, cos_ref, sin_ref, *out_refs,
                with_q, rope):
    d = D_MODEL
    mods = mods_ref[0]
    sh1, sc1 = mods[0:1], mods[1:2]
    h = (_rms(x_ref[0], gpre_ref[...]) * (1.0 + sc1) + sh1).astype(BF16)
    proj = _bdot(h, w_ref[...])
    c = V7X_LANES
    lane = lax.broadcasted_iota(jnp.int32, (1, c), 1)
    first_half = (lane % (HEAD_DIM // 2)) < (HEAD_DIM // 4)
    cos = cos_ref[...]
    sin = sin_ref[...]
    n_rot = 2 if with_q else 1
    for a in range(n_rot):
        scale = HEAD_DIM ** -0.5 * math.log2(math.e) if (with_q and a == 0) else 1.0
        for j in range(d // c):
            ch = proj[:, a * d + j * c: a * d + (j + 1) * c]
            if rope:
                partner = jnp.where(first_half,
                                    pltpu.roll(ch, c - HEAD_DIM // 4, axis=1),
                                    pltpu.roll(ch, HEAD_DIM // 4, axis=1))
                ch = ch * cos + partner * sin
            if scale != 1.0:
                ch = ch * scale
            out_refs[a][0, :, j * c:(j + 1) * c] = ch.astype(BF16)
    out_refs[n_rot][0] = lax.dot_general(wvt_ref[...], h, NT_DIMS,
                                         preferred_element_type=F32).astype(BF16)


def _qkv(x, mods, gpre, w, wvt, cos, sin, with_q, rope):
    bsz, n, d = x.shape
    ts = min(SEQ_TILE, n)
    n_rot = 2 if with_q else 1
    tok = pl.BlockSpec((1, ts, d), lambda b, t: (b, t, 0))
    tab = pl.BlockSpec((ts, V7X_LANES), lambda b, t: (t, 0))
    return pl.pallas_call(
        functools.partial(_qkv_kernel, with_q=with_q, rope=rope),
        grid=(bsz, n // ts),
        in_specs=[tok, _mods_spec(mods), _const_spec((1, d)), _const_spec((d, n_rot * d)),
                  _const_spec((d, d)), tab, tab],
        out_specs=[tok] * n_rot + [pl.BlockSpec((1, d, ts), lambda b, t: (b, 0, t))],
        out_shape=[jax.ShapeDtypeStruct((bsz, n, d), BF16)] * n_rot
        + [jax.ShapeDtypeStruct((bsz, d, n), BF16)],
        compiler_params=_params(2),
        name="qkv",
    )(x, mods, gpre, w, wvt, cos, sin)


def _attn_kernel(q_ref, kl_ref, vlt_ref, kc_ref, vct_ref, lam_ref, g_ref, o_ref, s_ref,
                 *, lam_init):
    lm = lam_ref[...]
    lam = (jnp.exp(jnp.sum(lm[0:1] * lm[1:2], axis=1, keepdims=True))
           - jnp.exp(jnp.sum(lm[2:3] * lm[3:4], axis=1, keepdims=True)) + lam_init)
    n, nc = kl_ref.shape[1], kc_ref.shape[1]
    sub = V7X_SUBLANES
    tq = ATTN_SUB_Q
    segs = [(kc_ref, vct_ref, 0, nc, 0)]
    segs += [(kl_ref, vlt_ref, j, ATTN_K_CHUNK, nc + j) for j in range(0, n, ATTN_K_CHUNK)]
    lane = lax.broadcasted_iota(jnp.int32, (1, V_DIM), 1)

    def score_seg(chain, qc, seg, m8):
        k_ref, _, off, rows, base = seg
        s = lax.dot_general(k_ref[0, off:off + rows, :], qc, NT_DIMS,
                            preferred_element_type=F32)
        s_ref[chain % 2, base:base + rows, :] = s
        part = jnp.max(s.reshape(rows // sub, sub, tq), axis=0)
        return part if m8 is None else jnp.maximum(m8, part)

    def exp_seg(chain, col_max, seg, acc, l8):
        _, vt_ref, off, rows, base = seg
        e = jnp.exp2(s_ref[chain % 2, base:base + rows, :] - col_max)
        l8 = l8 + jnp.sum(e.reshape(rows // sub, sub, tq), axis=0)
        return acc + _bdot(vt_ref[0, :, off:off + rows], e.astype(BF16)), l8

    n_chains = 2 * (q_ref.shape[1] // tq)
    heads = []
    prev = None
    for chain in range(n_chains + 1):
        if chain < n_chains:
            t, comp = divmod(chain, 2)
            in_comp = (lane >= comp * HEAD_DIM) & (lane < (comp + 1) * HEAD_DIM)
            q = q_ref[0, t * tq:(t + 1) * tq, :]
            qc = jnp.where(in_comp, q, jnp.zeros_like(q))
        m8 = None
        acc = jnp.zeros((V_DIM, tq), F32)
        l8 = jnp.zeros((sub, tq), F32)
        for seg in segs:
            if chain < n_chains:
                m8 = score_seg(chain, qc, seg, m8)
            if prev is not None:
                acc, l8 = exp_seg(prev[0], prev[1], seg, acc, l8)
        if prev is not None:
            heads.append(acc / jnp.sum(l8, axis=0, keepdims=True))
            if len(heads) == 2:
                t = prev[0] // 2
                o = jnp.transpose(heads[0] - lam * heads[1])
                o_ref[0, t * tq:(t + 1) * tq, :] = (
                    _rms(o, g_ref[...]) * (1.0 - lam_init)).astype(BF16)
                heads = []
        prev = (chain, jnp.max(m8, axis=0, keepdims=True)) if chain < n_chains else None


def _attention(q, k_l, vt_l, k_c, vt_c, lam_rows, subln_g, lam_init):
    bsz, n, d = q.shape
    nc = k_c.shape[1]
    tq = ATTN_Q_TILE
    hw = V_DIM
    return pl.pallas_call(
        functools.partial(_attn_kernel, lam_init=lam_init),
        grid=(bsz, N_HEADS, n // tq),
        in_specs=[
            pl.BlockSpec((1, tq, hw), lambda b, h, t: (b, t, h)),
            pl.BlockSpec((1, n, hw), lambda b, h, t: (b, 0, h)),
            pl.BlockSpec((1, hw, n), lambda b, h, t: (b, h, 0)),
            pl.BlockSpec((1, nc, hw), lambda b, h, t: (b, 0, h)),
            pl.BlockSpec((1, hw, nc), lambda b, h, t: (b, h, 0)),
            _const_spec((4, HEAD_DIM)),
            _const_spec((1, hw)),
        ],
        out_specs=pl.BlockSpec((1, tq, hw), lambda b, h, t: (b, t, h)),
        out_shape=jax.ShapeDtypeStruct((bsz, n, d), BF16),
        scratch_shapes=[pltpu.VMEM((2, nc + n, ATTN_SUB_Q), F32)],
        compiler_params=_params(3),
        name="diff_attn",
    )(q, k_l, vt_l, k_c, vt_c, lam_rows, subln_g)


def _oproj_kernel(o_ref, x_ref, mods_ref, gpost_ref, gffn_ref, wo_ref, wr_ref,
                  xo_ref, h2_ref, aff_ref):
    m = _bdot(o_ref[0], wo_ref[...])
    _mixer_tail(x_ref[0], m, mods_ref[0], gpost_ref, gffn_ref, wr_ref, xo_ref, h2_ref, aff_ref)


def _oproj(o, x, mods, gpost, gffn, w_o, wr_t):
    bsz, n, d = x.shape
    ts = min(SEQ_TILE, n)
    tok = pl.BlockSpec((1, ts, d), lambda b, t: (b, t, 0))
    out_specs, out_shapes = _tail_out(bsz, n, ts)
    return pl.pallas_call(
        _oproj_kernel,
        grid=(bsz, n // ts),
        in_specs=[tok, tok, _mods_spec(mods), _const_spec((1, d)), _const_spec((1, d)),
                  _const_spec((d, d)), _const_spec((N_EXPERTS, d))],
        out_specs=out_specs,
        out_shape=out_shapes,
        compiler_params=_params(2),
        name="oproj",
    )(o, x, mods, gpost, gffn, w_o, wr_t)


def _rope_tables(n):
    pos = jnp.arange(n)
    row = (pos // GRID_W).astype(F32)
    col = (pos % GRID_W).astype(F32)
    n_freq = HEAD_DIM // 4
    inv = 1.0 / (ROPE_BASE ** (jnp.arange(n_freq, dtype=F32) / n_freq))
    ar = row[:, None] * inv
    ac = col[:, None] * inv
    cos64 = jnp.concatenate([jnp.cos(ar), jnp.cos(ar), jnp.cos(ac), jnp.cos(ac)], axis=1)
    sin64 = jnp.concatenate([-jnp.sin(ar), jnp.sin(ar), -jnp.sin(ac), jnp.sin(ac)], axis=1)
    reps = V7X_LANES // HEAD_DIM
    return jnp.tile(cos64, (1, reps)), jnp.tile(sin64, (1, reps))


def kernel(x, c, ctx, c_ctx, ada_w, ada_b, pre_mix_g, post_mix_g, pre_ffn_g, post_ffn_g,
           conv_w_in, conv_k, conv_w_out, attn_w_qkv, attn_lambda_q1, attn_lambda_k1,
           attn_lambda_q2, attn_lambda_k2, attn_subln_g, attn_w_o, router_w,
           moe_w_gate, moe_w_up, moe_w_down):
    bsz, n, d = x.shape
    depth = ada_w.shape[0]
    assert (depth, d) == (2, D_MODEL) and n % SEQ_TILE == 0 and n % GRID_W == 0

    pad = (-(bsz + 1)) % V7X_SUBLANES
    cond = jnp.concatenate([c, c_ctx[None, :], jnp.zeros((pad, d), F32)], axis=0)
    ada = _ada(cond, ada_w, ada_b)
    mods_l = [ada[i, :bsz].reshape(bsz, N_MOD, d) for i in range(depth)]
    mods_c = [ada[i, bsz:bsz + 1].reshape(1, N_MOD, d) for i in range(depth)]
    row = lambda g, i: g[i].reshape(1, d)
    wr_t = [jnp.swapaxes(router_w[i], 0, 1) for i in range(depth)]
    wg, wu, wd = (w.astype(BF16) for w in (moe_w_gate, moe_w_up, moe_w_down))

    conv_args = (row(pre_mix_g, 0), row(post_mix_g, 0), row(pre_ffn_g, 0),
                 conv_w_in[0].astype(BF16), conv_k[0], conv_w_out[0].astype(BF16), wr_t[0])
    x_l, h2_l, aff_l = _conv_mixer(x, mods_l[0], *conv_args)
    x_c, h2_c, aff_c = _conv_mixer(ctx, mods_c[0], *conv_args)
    gp0 = row(post_ffn_g, 0)
    x_l = _ec_moe(x_l, h2_l, aff_l, mods_l[0], gp0, wg, wu, wd, 0, bb=1)
    x_c = _ec_moe(x_c, h2_c, aff_c, mods_c[0], gp0, wg, wu, wd, 0, bb=bsz)

    lam_init = 0.8 - 0.6 * math.exp(-0.3 * 1)
    cos, sin = _rope_tables(n)
    w_qkv = attn_w_qkv[0].astype(BF16)
    w_qk, w_k, w_vt = w_qkv[:, :2 * d], w_qkv[:, d:2 * d], jnp.swapaxes(w_qkv[:, 2 * d:], 0, 1)
    gpre1 = row(pre_mix_g, 1)
    q, k_l, vt_l = _qkv(x_l, mods_l[1], gpre1, w_qk, w_vt, cos, sin, with_q=True, rope=True)
    nc = ctx.shape[1]
    k_c, vt_c = _qkv(x_c, mods_c[1], gpre1, w_k, w_vt, cos[:nc], sin[:nc],
                     with_q=False, rope=False)
    lam_rows = jnp.stack([attn_lambda_q1[0], attn_lambda_k1[0],
                          attn_lambda_q2[0], attn_lambda_k2[0]], axis=0)
    o = _attention(q, k_l, vt_l, k_c, vt_c, lam_rows, attn_subln_g[0].reshape(1, V_DIM),
                   lam_init)
    x_l, h2_l, aff_l = _oproj(o, x_l, mods_l[1], row(post_mix_g, 1), row(pre_ffn_g, 1),
                              attn_w_o[0].astype(BF16), wr_t[1])
    return _ec_moe(x_l, h2_l, aff_l, mods_l[1], row(post_ffn_g, 1), wg, wu, wd, 1, bb=1)
```

```python
import functools
import math

import jax
import jax.numpy as jnp
from jax import lax
from jax.experimental import pallas as pl
from jax.experimental.pallas import tpu as pltpu

D_MODEL = 1024
N_EXPERTS = 16
EXPERT_FF = 2048
EC_FACTOR = 2
N_HEADS = 8
HEAD_DIM = 64
V_DIM = 2 * HEAD_DIM
GRID_W = 64
ROPE_BASE = 10000.0
N_MOD = 6
EPS = 1e-6

V7X_LANES = 128
V7X_SUBLANES = 8
V7X_VMEM_BYTES = 64 * 1024 * 1024
VMEM_LIMIT = V7X_VMEM_BYTES - 8 * 1024 * 1024

SEQ_TILE = 512
TOPK_SAMPLES_PER_STEP = 4
CTX_FF_CHUNKS = 8
ATTN_Q_TILE = 1024
ATTN_SUB_Q = 256
ATTN_K_CHUNK = 512
HALO = V7X_SUBLANES

F32 = jnp.float32
BF16 = jnp.bfloat16


def _params(n_axes):
    return pltpu.CompilerParams(
        dimension_semantics=("parallel",) * n_axes, vmem_limit_bytes=VMEM_LIMIT)


def _rms(x, g):
    return x * lax.rsqrt(jnp.mean(x * x, axis=-1, keepdims=True) + EPS) * g


def _silu(x):
    return x * (1.0 / (1.0 + jnp.exp(-x)))


def _bdot(a, b):
    return jnp.dot(a, b, preferred_element_type=F32)


def _ada_kernel(cond_ref, w_ref, b_ref, o_ref):
    s = _silu(cond_ref[...])
    o_ref[0] = jnp.dot(s, w_ref[0], precision=lax.Precision.HIGHEST,
                       preferred_element_type=F32) + b_ref[0]


def _ada(cond, ada_w, ada_b):
    depth, d, nd = ada_w.shape
    rows = cond.shape[0]
    tn = nd // 4
    return pl.pallas_call(
        _ada_kernel,
        grid=(depth, nd // tn),
        in_specs=[
            pl.BlockSpec((rows, d), lambda i, j: (0, 0)),
            pl.BlockSpec((1, d, tn), lambda i, j: (i, 0, j)),
            pl.BlockSpec((1, 1, tn), lambda i, j: (i, 0, j)),
        ],
        out_specs=pl.BlockSpec((1, rows, tn), lambda i, j: (i, 0, j)),
        out_shape=jax.ShapeDtypeStruct((depth, rows, nd), F32),
        compiler_params=_params(2),
        name="ada",
    )(cond, ada_w, ada_b.reshape(depth, 1, nd))


def _mixer_tail(x, m, mods, gpost_ref, gffn_ref, wr_ref, xo_ref, h2_ref, aff_ref):
    g1, sh2, sc2 = mods[2:3], mods[3:4], mods[4:5]
    xn = x + g1 * _rms(m, gpost_ref[...])
    xo_ref[0] = xn
    h2 = _rms(xn, gffn_ref[...]) * (1.0 + sc2) + sh2
    h2_ref[0] = h2.astype(BF16)
    logits = lax.dot_general(wr_ref[...], h2, (((1,), (1,)), ((), ())),
                             precision=lax.Precision.HIGHEST, preferred_element_type=F32)
    ex = jnp.exp(logits - jnp.max(logits, axis=0, keepdims=True))
    aff_ref[0] = ex / jnp.sum(ex, axis=0, keepdims=True)


def _conv_kernel(x_ref, xp_ref, xn_ref, mods_ref, gpre_ref, gpost_ref, gffn_ref,
                 win_ref, ck_ref, wout_ref, wr_ref, xo_ref, h2_ref, aff_ref, *, ts):
    d = D_MODEL
    t = pl.program_id(1)
    last = pl.num_programs(1) - 1
    mods = mods_ref[0]
    sh1, sc1 = mods[0:1], mods[1:2]
    x = x_ref[0]
    rows = ts + 2 * HALO
    xe = jnp.concatenate([xp_ref[0], x, xn_ref[0]], axis=0)
    h = _rms(xe, gpre_ref[...]) * (1.0 + sc1) + sh1
    proj = _bdot(h.astype(BF16), win_ref[...])
    u = proj[:, d:2 * d] * proj[:, 2 * d:]
    r = lax.broadcasted_iota(jnp.int32, (rows, 1), 0)
    inside = jnp.logical_and(jnp.logical_or(r >= HALO, t > 0),
                             jnp.logical_or(r < ts + HALO, t < last))
    u = jnp.where(inside, u, 0.0)
    ck = ck_ref[...]
    u_prev = pltpu.roll(u, 1, axis=0)[HALO:HALO + ts]
    u_next = pltpu.roll(u, rows - 1, axis=0)[HALO:HALO + ts]
    conv = ck[0:1] * u_prev + ck[1:2] * u[HALO:HALO + ts] + ck[2:3] * u_next
    z = proj[HALO:HALO + ts, :d] * conv
    m = _bdot(z.astype(BF16), wout_ref[...])
    _mixer_tail(x, m, mods, gpost_ref, gffn_ref, wr_ref, xo_ref, h2_ref, aff_ref)


def _const_spec(shape):
    return pl.BlockSpec(shape, lambda *_: (0,) * len(shape))


def _mods_spec(mods):
    if mods.shape[0] == 1:
        return pl.BlockSpec((1, N_MOD, D_MODEL), lambda b, t: (0, 0, 0))
    return pl.BlockSpec((1, N_MOD, D_MODEL), lambda b, t: (b, 0, 0))


def _tail_out(bsz, n, ts):
    d, e = D_MODEL, N_EXPERTS
    specs = [
        pl.BlockSpec((1, ts, d), lambda b, t: (b, t, 0)),
        pl.BlockSpec((1, ts, d), lambda b, t: (b, t, 0)),
        pl.BlockSpec((1, e, ts), lambda b, t: (b, 0, t)),
    ]
    shapes = [
        jax.ShapeDtypeStruct((bsz, n, d), F32),
        jax.ShapeDtypeStruct((bsz, n, d), BF16),
        jax.ShapeDtypeStruct((bsz, e, n), F32),
    ]
    return specs, shapes


def _conv_mixer(x, mods, gpre, gpost, gffn, w_in, ck, w_out, wr_t):
    bsz, n, d = x.shape
    ts = min(SEQ_TILE, n)
    nblk = n // HALO
    per = ts // HALO
    out_specs, out_shapes = _tail_out(bsz, n, ts)
    return pl.pallas_call(
        functools.partial(_conv_kernel, ts=ts),
        grid=(bsz, n // ts),
        in_specs=[
            pl.BlockSpec((1, ts, d), lambda b, t: (b, t, 0)),
            pl.BlockSpec((1, HALO, d), lambda b, t: (b, jnp.maximum(t * per - 1, 0), 0)),
            pl.BlockSpec((1, HALO, d), lambda b, t: (b, jnp.minimum((t + 1) * per, nblk - 1), 0)),
            _mods_spec(mods),
            _const_spec((1, d)), _const_spec((1, d)), _const_spec((1, d)),
            _const_spec((d, 3 * d)),
            _const_spec((3, d)),
            _const_spec((d, d)),
            _const_spec((N_EXPERTS, d)),
        ],
        out_specs=out_specs,
        out_shape=out_shapes,
        compiler_params=_params(2),
        name="conv_mixer",
    )(x, x, x, mods, gpre, gpost, gffn, w_in, ck, w_out, wr_t)


def _excl_cumsum(mask_f):
    rows, n = mask_f.shape
    c = V7X_LANES
    ri = lax.broadcasted_iota(jnp.int32, (c, c), 0)
    ci = lax.broadcasted_iota(jnp.int32, (c, c), 1)
    tri = jnp.where(ri < ci, 1.0, 0.0).astype(BF16)
    ones = jnp.ones((c, c), BF16)
    carry = jnp.zeros((rows, c), F32)
    outs = []
    for j in range(n // c):
        ch = mask_f[:, j * c:(j + 1) * c].astype(BF16)
        outs.append(_bdot(ch, tri) + carry)
        carry = carry + _bdot(ch, ones)
    return jnp.concatenate(outs, axis=1)


def _topk_kernel(aff_ref, slot_ref, *, cap):
    a = aff_ref[0]
    rows = a.shape[0]
    thr = jnp.zeros((rows, 1), jnp.int32)
    capf = float(cap)
    for bit in range(30, -1, -1):
        cand = thr | (1 << bit)
        cand_f = lax.bitcast_convert_type(cand, F32)
        cnt = jnp.sum(jnp.where(a >= cand_f, 1.0, 0.0), axis=1, keepdims=True)
        thr = jnp.where(cnt >= capf, cand, thr)
    thr_f = lax.bitcast_convert_type(thr, F32)
    gt = jnp.where(a > thr_f, 1.0, 0.0)
    eq = jnp.where(a == thr_f, 1.0, 0.0)
    need = capf - jnp.sum(gt, axis=1, keepdims=True)
    sel = gt + eq * jnp.where(_excl_cumsum(eq) < need, 1.0, 0.0)
    slot = _excl_cumsum(sel)
    slot_ref[0] = jnp.where(sel > 0.0, slot, -1.0).astype(jnp.int32)


def _topk(aff_t, cap):
    bsz, e, n = aff_t.shape
    group = math.gcd(bsz, TOPK_SAMPLES_PER_STEP)
    rows = group * e
    slot = pl.pallas_call(
        functools.partial(_topk_kernel, cap=cap),
        grid=(bsz // group,),
        in_specs=[pl.BlockSpec((1, rows, n), lambda b: (b, 0, 0))],
        out_specs=pl.BlockSpec((1, rows, n), lambda b: (b, 0, 0)),
        out_shape=jax.ShapeDtypeStruct((bsz // group, rows, n), jnp.int32),
        compiler_params=_params(1),
        name="topk",
    )(aff_t.reshape(bsz // group, rows, n))
    return slot.reshape(bsz, e, n)


def _slot_hits(slot_row, cap):
    return slot_row == lax.broadcasted_iota(jnp.int32, (cap, slot_row.shape[1]), 0)


def _gather_rows(hit, h):
    return _bdot(jnp.where(hit, 1.0, 0.0).astype(BF16), h).astype(BF16)


def _gather_vals(hit, aff_row):
    return jnp.sum(jnp.where(hit, aff_row, 0.0), axis=1, keepdims=True)


def _expert_kernel(slot_ref, aff_ref, h_ref, wg_ref, wu_ref, wd_ref, ys_ref,
                   wg_s, wu_s, wd_s, *, cap):
    r = pl.program_id(0)
    b = pl.program_id(1)
    n_exp = pl.num_programs(0) - 1
    fill = r % 2
    rows_in = wg_ref.shape[2]
    rows_dn = wd_ref.shape[2]

    @pl.when(r < n_exp)
    def _():
        o_in = pl.multiple_of(b * rows_in, rows_in)
        o_dn = pl.multiple_of(b * rows_dn, rows_dn)
        wg_s[fill, pl.ds(o_in, rows_in), :] = wg_ref[0, 0].astype(BF16)
        wu_s[fill, pl.ds(o_in, rows_in), :] = wu_ref[0, 0].astype(BF16)
        wd_s[fill, pl.ds(o_dn, rows_dn), :] = wd_ref[0, 0].astype(BF16)

    @pl.when(r == 0)
    def _():
        ys_ref[0, 0] = jnp.zeros(ys_ref.shape[2:], BF16)

    @pl.when(r > 0)
    def _():
        use = 1 - fill
        hit = _slot_hits(slot_ref[0, 0], cap)
        xs = _gather_rows(hit, h_ref[0])
        g = _bdot(xs, wg_s[use])
        u = _bdot(xs, wu_s[use])
        y = _bdot((_silu(g) * u).astype(BF16), wd_s[use])
        ys_ref[0, 0] = (y * _gather_vals(hit, aff_ref[0, 0])).astype(BF16)


def _experts(slot_t, aff_t, h2, wg, wu, wd, layer, cap):
    bsz, e, n = slot_t.shape
    d, f = D_MODEL, EXPERT_FF
    assert d % bsz == 0 and f % bsz == 0
    cur = lambda r, b: (b, jnp.maximum(r - 1, 0), 0, 0)
    nxt = lambda r, b: (layer, jnp.minimum(r, e - 1), b, 0)
    return pl.pallas_call(
        functools.partial(_expert_kernel, cap=cap),
        grid=(e + 1, bsz),
        in_specs=[
            pl.BlockSpec((1, 1, 1, n), cur),
            pl.BlockSpec((1, 1, 1, n), cur),
            pl.BlockSpec((1, n, d), lambda r, b: (b, 0, 0)),
            pl.BlockSpec((1, 1, d // bsz, f), nxt),
            pl.BlockSpec((1, 1, d // bsz, f), nxt),
            pl.BlockSpec((1, 1, f // bsz, d), nxt),
        ],
        out_specs=pl.BlockSpec((1, 1, cap, d),
                               lambda r, b: (b, jnp.where(r == 0, e, r - 1), 0, 0)),
        out_shape=jax.ShapeDtypeStruct((bsz, e + 1, cap, d), BF16),
        scratch_shapes=[pltpu.VMEM((2, d, f), BF16), pltpu.VMEM((2, d, f), BF16),
                        pltpu.VMEM((2, f, d), BF16)],
        compiler_params=pltpu.CompilerParams(
            dimension_semantics=("arbitrary", "arbitrary"), vmem_limit_bytes=VMEM_LIMIT),
        name="experts",
    )(slot_t.reshape(bsz, e, 1, n), aff_t.reshape(bsz, e, 1, n), h2, wg, wu, wd)


def _expert_stream_kernel(slot_ref, aff_ref, h_ref, wg_ref, wu_ref, wd_ref, ys_ref,
                          xs_ref, y_ref, *, cap):
    c = pl.program_id(1)
    bsz = h_ref.shape[0]

    @pl.when(c == 0)
    def _():
        for i in range(bsz):
            xs_ref[i * cap:(i + 1) * cap, :] = _gather_rows(
                _slot_hits(slot_ref[i, 0], cap), h_ref[i])
        y_ref[...] = jnp.zeros(y_ref.shape, F32)

    xs = xs_ref[...]
    g = _bdot(xs, wg_ref[0, 0].astype(BF16))
    u = _bdot(xs, wu_ref[0, 0].astype(BF16))
    y_ref[...] += _bdot((_silu(g) * u).astype(BF16), wd_ref[0, 0].astype(BF16))

    @pl.when(c == pl.num_programs(1) - 1)
    def _():
        for i in range(bsz):
            vals = _gather_vals(_slot_hits(slot_ref[i, 0], cap), aff_ref[i, 0])
            ys_ref[i, 0] = (y_ref[i * cap:(i + 1) * cap, :] * vals).astype(BF16)


def _experts_stream(slot_t, aff_t, h2, wg, wu, wd, layer, cap):
    bsz, e, n = slot_t.shape
    d, f = D_MODEL, EXPERT_FF
    fc = f // CTX_FF_CHUNKS
    row = lambda ei, c: (0, ei, 0, 0)
    return pl.pallas_call(
        functools.partial(_expert_stream_kernel, cap=cap),
        grid=(e, CTX_FF_CHUNKS),
        in_specs=[
            pl.BlockSpec((bsz, 1, 1, n), row),
            pl.BlockSpec((bsz, 1, 1, n), row),
            pl.BlockSpec((bsz, n, d), lambda ei, c: (0, 0, 0)),
            pl.BlockSpec((1, 1, d, fc), lambda ei, c: (layer, ei, 0, c)),
            pl.BlockSpec((1, 1, d, fc), lambda ei, c: (layer, ei, 0, c)),
            pl.BlockSpec((1, 1, fc, d), lambda ei, c: (layer, ei, c, 0)),
        ],
        out_specs=pl.BlockSpec((bsz, 1, cap, d), row),
        out_shape=jax.ShapeDtypeStruct((bsz, e, cap, d), BF16),
        scratch_shapes=[pltpu.VMEM((bsz * cap, d), BF16), pltpu.VMEM((bsz * cap, d), F32)],
        compiler_params=pltpu.CompilerParams(
            dimension_semantics=("parallel", "arbitrary"), vmem_limit_bytes=VMEM_LIMIT),
        name="experts_stream",
    )(slot_t.reshape(bsz, e, 1, n), aff_t.reshape(bsz, e, 1, n), h2, wg, wu, wd)


def _combine_kernel(slot_ref, ys_ref, x_ref, mods_ref, gpost_ref, xo_ref, *, cap):
    e = N_EXPERTS
    sl = slot_ref[0]
    tn = sl.shape[0]
    if cap % V7X_LANES == 0:
        lane = lax.broadcasted_iota(jnp.int32, (1, cap), 1)
        pt = jnp.concatenate(
            [jnp.where(sl[:, j:j + 1] == lane, 1.0, 0.0).astype(BF16) for j in range(e)],
            axis=1)
    else:
        lane = lax.broadcasted_iota(jnp.int32, (1, e * cap), 1)
        acc = jnp.zeros((tn, e * cap), F32)
        for j in range(e):
            col = sl[:, j:j + 1]
            acc = acc + jnp.where(jnp.where(col >= 0, col + j * cap, -1) == lane, 1.0, 0.0)
        pt = acc.astype(BF16)
    f = _bdot(pt, ys_ref[0])
    g2 = mods_ref[0][5:6]
    xo_ref[0] = x_ref[0] + g2 * _rms(f, gpost_ref[...])


def _combine(slot, ys, x, mods, gpost, cap):
    bsz, n, d = x.shape
    e = N_EXPERTS
    tn = min(SEQ_TILE, n)
    return pl.pallas_call(
        functools.partial(_combine_kernel, cap=cap),
        grid=(bsz, n // tn),
        in_specs=[
            pl.BlockSpec((1, tn, e), lambda b, t: (b, t, 0)),
            pl.BlockSpec((1, e * cap, d), lambda b, t: (b, 0, 0)),
            pl.BlockSpec((1, tn, d), lambda b, t: (b, t, 0)),
            _mods_spec(mods),
            _const_spec((1, d)),
        ],
        out_specs=pl.BlockSpec((1, tn, d), lambda b, t: (b, t, 0)),
        out_shape=jax.ShapeDtypeStruct((bsz, n, d), F32),
        compiler_params=_params(2),
        name="combine",
    )(slot, ys.reshape(bsz, -1, d), x, mods, gpost)


def _ec_moe(x, h2, aff_t, mods, gpost, wg, wu, wd, layer, stream):
    n = x.shape[1]
    cap = EC_FACTOR * n // N_EXPERTS
    slot_t = _topk(aff_t, cap)
    experts = _experts_stream if stream else _experts
    ys = experts(slot_t, aff_t, h2, wg, wu, wd, layer, cap)
    slot = jnp.swapaxes(slot_t, 1, 2)
    return _combine(slot, ys, x, mods, gpost, cap)


NT_DIMS = (((1,), (1,)), ((), ()))


def _qkv_kernel(x_ref, mods_ref, gpre_ref, w_ref, wvt_ref, cos_ref, sin_ref, *out_refs,
                with_q, rope):
    d = D_MODEL
    mods = mods_ref[0]
    sh1, sc1 = mods[0:1], mods[1:2]
    h = (_rms(x_ref[0], gpre_ref[...]) * (1.0 + sc1) + sh1).astype(BF16)
    proj = _bdot(h, w_ref[...])
    c = V7X_LANES
    lane = lax.broadcasted_iota(jnp.int32, (1, c), 1)
    first_half = (lane % (HEAD_DIM // 2)) < (HEAD_DIM // 4)
    cos = cos_ref[...]
    sin = sin_ref[...]
    n_rot = 2 if with_q else 1
    for a in range(n_rot):
        scale = HEAD_DIM ** -0.5 * math.log2(math.e) if (with_q and a == 0) else 1.0
        for j in range(d // c):
            ch = proj[:, a * d + j * c: a * d + (j + 1) * c]
            if rope:
                partner = jnp.where(first_half,
                                    pltpu.roll(ch, c - HEAD_DIM // 4, axis=1),
                                    pltpu.roll(ch, HEAD_DIM // 4, axis=1))
                ch = ch * cos + partner * sin
            if scale != 1.0:
                ch = ch * scale
            out_refs[a][0, :, j * c:(j + 1) * c] = ch.astype(BF16)
    out_refs[n_rot][0] = lax.dot_general(wvt_ref[...], h, NT_DIMS,
                                         preferred_element_type=F32).astype(BF16)


def _qkv(x, mods, gpre, w, wvt, cos, sin, with_q, rope):
    bsz, n, d = x.shape
    ts = min(SEQ_TILE, n)
    n_rot = 2 if with_q else 1
    tok = pl.BlockSpec((1, ts, d), lambda b, t: (b, t, 0))
    tab = pl.BlockSpec((ts, V7X_LANES), lambda b, t: (t, 0))
    return pl.pallas_call(
        functools.partial(_qkv_kernel, with_q=with_q, rope=rope),
        grid=(bsz, n // ts),
        in_specs=[tok, _mods_spec(mods), _const_spec((1, d)), _const_spec((d, n_rot * d)),
                  _const_spec((d, d)), tab, tab],
        out_specs=[tok] * n_rot + [pl.BlockSpec((1, d, ts), lambda b, t: (b, 0, t))],
        out_shape=[jax.ShapeDtypeStruct((bsz, n, d), BF16)] * n_rot
        + [jax.ShapeDtypeStruct((bsz, d, n), BF16)],
        compiler_params=_params(2),
        name="qkv",
    )(x, mods, gpre, w, wvt, cos, sin)


def _attn_kernel(q_ref, kl_ref, vlt_ref, kc_ref, vct_ref, lam_ref, g_ref, o_ref, s_ref,
                 *, lam_init):
    lm = lam_ref[...]
    lam = (jnp.exp(jnp.sum(lm[0:1] * lm[1:2], axis=1, keepdims=True))
           - jnp.exp(jnp.sum(lm[2:3] * lm[3:4], axis=1, keepdims=True)) + lam_init)
    n, nc = kl_ref.shape[1], kc_ref.shape[1]
    sub = V7X_SUBLANES
    tq = ATTN_SUB_Q
    segs = [(kc_ref, vct_ref, 0, nc, 0)]
    segs += [(kl_ref, vlt_ref, j, ATTN_K_CHUNK, nc + j) for j in range(0, n, ATTN_K_CHUNK)]
    lane = lax.broadcasted_iota(jnp.int32, (1, V_DIM), 1)

    def score_seg(chain, qc, seg, m8):
        k_ref, _, off, rows, base = seg
        s = lax.dot_general(k_ref[0, off:off + rows, :], qc, NT_DIMS,
                            preferred_element_type=F32)
        s_ref[chain % 2, base:base + rows, :] = s
        part = jnp.max(s.reshape(rows // sub, sub, tq), axis=0)
        return part if m8 is None else jnp.maximum(m8, part)

    def exp_seg(chain, col_max, seg, acc, l8):
        _, vt_ref, off, rows, base = seg
        e = jnp.exp2(s_ref[chain % 2, base:base + rows, :] - col_max)
        l8 = l8 + jnp.sum(e.reshape(rows // sub, sub, tq), axis=0)
        return acc + _bdot(vt_ref[0, :, off:off + rows], e.astype(BF16)), l8

    n_chains = 2 * (q_ref.shape[1] // tq)
    heads = []
    prev = None
    for chain in range(n_chains + 1):
        if chain < n_chains:
            t, comp = divmod(chain, 2)
            in_comp = (lane >= comp * HEAD_DIM) & (lane < (comp + 1) * HEAD_DIM)
            q = q_ref[0, t * tq:(t + 1) * tq, :]
            qc = jnp.where(in_comp, q, jnp.zeros_like(q))
        m8 = None
        acc = jnp.zeros((V_DIM, tq), F32)
        l8 = jnp.zeros((sub, tq), F32)
        for seg in segs:
            if chain < n_chains:
                m8 = score_seg(chain, qc, seg, m8)
            if prev is not None:
                acc, l8 = exp_seg(prev[0], prev[1], seg, acc, l8)
        if prev is not None:
            heads.append(acc / jnp.sum(l8, axis=0, keepdims=True))
            if len(heads) == 2:
                t = prev[0] // 2
                o = jnp.transpose(heads[0] - lam * heads[1])
                o_ref[0, t * tq:(t + 1) * tq, :] = (
                    _rms(o, g_ref[...]) * (1.0 - lam_init)).astype(BF16)
                heads = []
        prev = (chain, jnp.max(m8, axis=0, keepdims=True)) if chain < n_chains else None


def _attention(q, k_l, vt_l, k_c, vt_c, lam_rows, subln_g, lam_init):
    bsz, n, d = q.shape
    nc = k_c.shape[1]
    tq = ATTN_Q_TILE
    hw = V_DIM
    return pl.pallas_call(
        functools.partial(_attn_kernel, lam_init=lam_init),
        grid=(bsz, N_HEADS, n // tq),
        in_specs=[
            pl.BlockSpec((1, tq, hw), lambda b, h, t: (b, t, h)),
            pl.BlockSpec((1, n, hw), lambda b, h, t: (b, 0, h)),
            pl.BlockSpec((1, hw, n), lambda b, h, t: (b, h, 0)),
            pl.BlockSpec((1, nc, hw), lambda b, h, t: (b, 0, h)),
            pl.BlockSpec((1, hw, nc), lambda b, h, t: (b, h, 0)),
            _const_spec((4, HEAD_DIM)),
            _const_spec((1, hw)),
        ],
        out_specs=pl.BlockSpec((1, tq, hw), lambda b, h, t: (b, t, h)),
        out_shape=jax.ShapeDtypeStruct((bsz, n, d), BF16),
        scratch_shapes=[pltpu.VMEM((2, nc + n, ATTN_SUB_Q), F32)],
        compiler_params=_params(3),
        name="diff_attn",
    )(q, k_l, vt_l, k_c, vt_c, lam_rows, subln_g)


def _oproj_kernel(o_ref, x_ref, mods_ref, gpost_ref, gffn_ref, wo_ref, wr_ref,
                  xo_ref, h2_ref, aff_ref):
    m = _bdot(o_ref[0], wo_ref[...])
    _mixer_tail(x_ref[0], m, mods_ref[0], gpost_ref, gffn_ref, wr_ref, xo_ref, h2_ref, aff_ref)


def _oproj(o, x, mods, gpost, gffn, w_o, wr_t):
    bsz, n, d = x.shape
    ts = min(SEQ_TILE, n)
    tok = pl.BlockSpec((1, ts, d), lambda b, t: (b, t, 0))
    out_specs, out_shapes = _tail_out(bsz, n, ts)
    return pl.pallas_call(
        _oproj_kernel,
        grid=(bsz, n // ts),
        in_specs=[tok, tok, _mods_spec(mods), _const_spec((1, d)), _const_spec((1, d)),
                  _const_spec((d, d)), _const_spec((N_EXPERTS, d))],
        out_specs=out_specs,
        out_shape=out_shapes,
        compiler_params=_params(2),
        name="oproj",
    )(o, x, mods, gpost, gffn, w_o, wr_t)


def _rope_tables(n):
    pos = jnp.arange(n)
    row = (pos // GRID_W).astype(F32)
    col = (pos % GRID_W).astype(F32)
    n_freq = HEAD_DIM // 4
    inv = 1.0 / (ROPE_BASE ** (jnp.arange(n_freq, dtype=F32) / n_freq))
    ar = row[:, None] * inv
    ac = col[:, None] * inv
    cos64 = jnp.concatenate([jnp.cos(ar), jnp.cos(ar), jnp.cos(ac), jnp.cos(ac)], axis=1)
    sin64 = jnp.concatenate([-jnp.sin(ar), jnp.sin(ar), -jnp.sin(ac), jnp.sin(ac)], axis=1)
    reps = V7X_LANES // HEAD_DIM
    return jnp.tile(cos64, (1, reps)), jnp.tile(sin64, (1, reps))


def kernel(x, c, ctx, c_ctx, ada_w, ada_b, pre_mix_g, post_mix_g, pre_ffn_g, post_ffn_g,
           conv_w_in, conv_k, conv_w_out, attn_w_qkv, attn_lambda_q1, attn_lambda_k1,
           attn_lambda_q2, attn_lambda_k2, attn_subln_g, attn_w_o, router_w,
           moe_w_gate, moe_w_up, moe_w_down):
    bsz, n, d = x.shape
    depth = ada_w.shape[0]
    assert (depth, d) == (2, D_MODEL) and n % SEQ_TILE == 0 and n % GRID_W == 0

    pad = (-(bsz + 1)) % V7X_SUBLANES
    cond = jnp.concatenate([c, c_ctx[None, :], jnp.zeros((pad, d), F32)], axis=0)
    ada = _ada(cond, ada_w, ada_b)
    mods_l = [ada[i, :bsz].reshape(bsz, N_MOD, d) for i in range(depth)]
    mods_c = [ada[i, bsz:bsz + 1].reshape(1, N_MOD, d) for i in range(depth)]
    row = lambda g, i: g[i].reshape(1, d)
    wr_t = [jnp.swapaxes(router_w[i], 0, 1) for i in range(depth)]
    wg, wu, wd = moe_w_gate, moe_w_up, moe_w_down

    conv_args = (row(pre_mix_g, 0), row(post_mix_g, 0), row(pre_ffn_g, 0),
                 conv_w_in[0].astype(BF16), conv_k[0], conv_w_out[0].astype(BF16), wr_t[0])
    x_l, h2_l, aff_l = _conv_mixer(x, mods_l[0], *conv_args)
    x_c, h2_c, aff_c = _conv_mixer(ctx, mods_c[0], *conv_args)
    gp0 = row(post_ffn_g, 0)
    x_l = _ec_moe(x_l, h2_l, aff_l, mods_l[0], gp0, wg, wu, wd, 0, stream=False)
    x_c = _ec_moe(x_c, h2_c, aff_c, mods_c[0], gp0, wg, wu, wd, 0, stream=True)

    lam_init = 0.8 - 0.6 * math.exp(-0.3 * 1)
    cos, sin = _rope_tables(n)
    w_qkv = attn_w_qkv[0].astype(BF16)
    w_qk, w_k, w_vt = w_qkv[:, :2 * d], w_qkv[:, d:2 * d], jnp.swapaxes(w_qkv[:, 2 * d:], 0, 1)
    gpre1 = row(pre_mix_g, 1)
    q, k_l, vt_l = _qkv(x_l, mods_l[1], gpre1, w_qk, w_vt, cos, sin, with_q=True, rope=True)
    nc = ctx.shape[1]
    k_c, vt_c = _qkv(x_c, mods_c[1], gpre1, w_k, w_vt, cos[:nc], sin[:nc],
                     with_q=False, rope=False)
    lam_rows = jnp.stack([attn_lambda_q1[0], attn_lambda_k1[0],
                          attn_lambda_q2[0], attn_lambda_k2[0]], axis=0)
    o = _attention(q, k_l, vt_l, k_c, vt_c, lam_rows, attn_subln_g[0].reshape(1, V_DIM),
                   lam_init)
    x_l, h2_l, aff_l = _oproj(o, x_l, mods_l[1], row(post_mix_g, 1), row(pre_ffn_g, 1),
                              attn_w_o[0].astype(BF16), wr_t[1])
    return _ec_moe(x_l, h2_l, aff_l, mods_l[1], row(post_ffn_g, 1), wg, wu, wd, 1,
                   stream=False)
```

```python
import functools
import math

import jax
import jax.numpy as jnp
from jax import lax
from jax.experimental import pallas as pl
from jax.experimental.pallas import tpu as pltpu

D_MODEL = 1024
N_EXPERTS = 16
EXPERT_FF = 2048
EC_FACTOR = 2
N_HEADS = 8
HEAD_DIM = 64
V_DIM = 2 * HEAD_DIM
GRID_W = 64
ROPE_BASE = 10000.0
N_MOD = 6
EPS = 1e-6

V7X_LANES = 128
V7X_SUBLANES = 8
V7X_VMEM_BYTES = 64 * 1024 * 1024
VMEM_LIMIT = V7X_VMEM_BYTES - 8 * 1024 * 1024

SEQ_TILE = 512
TOPK_SAMPLES_PER_STEP = 4
CTX_FF_CHUNKS = 4
ATTN_Q_TILE = 2048
ATTN_SUB_Q = 256
ATTN_K_CHUNK = 512
HALO = V7X_SUBLANES

F32 = jnp.float32
BF16 = jnp.bfloat16


def _params(n_axes):
    return pltpu.CompilerParams(
        dimension_semantics=("parallel",) * n_axes, vmem_limit_bytes=VMEM_LIMIT)


def _rms(x, g):
    return x * lax.rsqrt(jnp.mean(x * x, axis=-1, keepdims=True) + EPS) * g


def _silu(x):
    return x * (1.0 / (1.0 + jnp.exp(-x)))


def _bdot(a, b):
    return jnp.dot(a, b, preferred_element_type=F32)


def _ada_kernel(cond_ref, w_ref, b_ref, o_ref):
    s = _silu(cond_ref[...])
    o_ref[0] = jnp.dot(s, w_ref[0], precision=lax.Precision.HIGHEST,
                       preferred_element_type=F32) + b_ref[0]


def _ada(cond, ada_w, ada_b):
    depth, d, nd = ada_w.shape
    rows = cond.shape[0]
    tn = nd // 4
    return pl.pallas_call(
        _ada_kernel,
        grid=(depth, nd // tn),
        in_specs=[
            pl.BlockSpec((rows, d), lambda i, j: (0, 0)),
            pl.BlockSpec((1, d, tn), lambda i, j: (i, 0, j)),
            pl.BlockSpec((1, 1, tn), lambda i, j: (i, 0, j)),
        ],
        out_specs=pl.BlockSpec((1, rows, tn), lambda i, j: (i, 0, j)),
        out_shape=jax.ShapeDtypeStruct((depth, rows, nd), F32),
        compiler_params=_params(2),
        name="ada",
    )(cond, ada_w, ada_b.reshape(depth, 1, nd))


def _mixer_tail(x, m, mods, gpost_ref, gffn_ref, wr_ref, xo_ref, h2_ref, aff_ref):
    g1, sh2, sc2 = mods[2:3], mods[3:4], mods[4:5]
    xn = x + g1 * _rms(m, gpost_ref[...])
    xo_ref[0] = xn
    h2 = _rms(xn, gffn_ref[...]) * (1.0 + sc2) + sh2
    h2_ref[0] = h2.astype(BF16)
    logits = lax.dot_general(wr_ref[...], h2, (((1,), (1,)), ((), ())),
                             precision=lax.Precision.HIGHEST, preferred_element_type=F32)
    ex = jnp.exp(logits - jnp.max(logits, axis=0, keepdims=True))
    aff_ref[0] = ex / jnp.sum(ex, axis=0, keepdims=True)


def _conv_kernel(x_ref, xp_ref, xn_ref, mods_ref, gpre_ref, gpost_ref, gffn_ref,
                 win_ref, ck_ref, wout_ref, wr_ref, xo_ref, h2_ref, aff_ref, *, ts):
    d = D_MODEL
    t = pl.program_id(1)
    last = pl.num_programs(1) - 1
    mods = mods_ref[0]
    sh1, sc1 = mods[0:1], mods[1:2]
    x = x_ref[0]
    rows = ts + 2 * HALO
    xe = jnp.concatenate([xp_ref[0], x, xn_ref[0]], axis=0)
    h = _rms(xe, gpre_ref[...]) * (1.0 + sc1) + sh1
    proj = _bdot(h.astype(BF16), win_ref[...])
    u = proj[:, d:2 * d] * proj[:, 2 * d:]
    r = lax.broadcasted_iota(jnp.int32, (rows, 1), 0)
    inside = jnp.logical_and(jnp.logical_or(r >= HALO, t > 0),
                             jnp.logical_or(r < ts + HALO, t < last))
    u = jnp.where(inside, u, 0.0)
    ck = ck_ref[...]
    u_prev = pltpu.roll(u, 1, axis=0)[HALO:HALO + ts]
    u_next = pltpu.roll(u, rows - 1, axis=0)[HALO:HALO + ts]
    conv = ck[0:1] * u_prev + ck[1:2] * u[HALO:HALO + ts] + ck[2:3] * u_next
    z = proj[HALO:HALO + ts, :d] * conv
    m = _bdot(z.astype(BF16), wout_ref[...])
    _mixer_tail(x, m, mods, gpost_ref, gffn_ref, wr_ref, xo_ref, h2_ref, aff_ref)


def _const_spec(shape):
    return pl.BlockSpec(shape, lambda *_: (0,) * len(shape))


def _mods_spec(mods):
    if mods.shape[0] == 1:
        return pl.BlockSpec((1, N_MOD, D_MODEL), lambda b, t: (0, 0, 0))
    return pl.BlockSpec((1, N_MOD, D_MODEL), lambda b, t: (b, 0, 0))


def _tail_out(bsz, n, ts):
    d, e = D_MODEL, N_EXPERTS
    specs = [
        pl.BlockSpec((1, ts, d), lambda b, t: (b, t, 0)),
        pl.BlockSpec((1, ts, d), lambda b, t: (b, t, 0)),
        pl.BlockSpec((1, e, ts), lambda b, t: (b, 0, t)),
    ]
    shapes = [
        jax.ShapeDtypeStruct((bsz, n, d), F32),
        jax.ShapeDtypeStruct((bsz, n, d), BF16),
        jax.ShapeDtypeStruct((bsz, e, n), F32),
    ]
    return specs, shapes


def _conv_mixer(x, mods, gpre, gpost, gffn, w_in, ck, w_out, wr_t):
    bsz, n, d = x.shape
    ts = min(SEQ_TILE, n)
    nblk = n // HALO
    per = ts // HALO
    out_specs, out_shapes = _tail_out(bsz, n, ts)
    return pl.pallas_call(
        functools.partial(_conv_kernel, ts=ts),
        grid=(bsz, n // ts),
        in_specs=[
            pl.BlockSpec((1, ts, d), lambda b, t: (b, t, 0)),
            pl.BlockSpec((1, HALO, d), lambda b, t: (b, jnp.maximum(t * per - 1, 0), 0)),
            pl.BlockSpec((1, HALO, d), lambda b, t: (b, jnp.minimum((t + 1) * per, nblk - 1), 0)),
            _mods_spec(mods),
            _const_spec((1, d)), _const_spec((1, d)), _const_spec((1, d)),
            _const_spec((d, 3 * d)),
            _const_spec((3, d)),
            _const_spec((d, d)),
            _const_spec((N_EXPERTS, d)),
        ],
        out_specs=out_specs,
        out_shape=out_shapes,
        compiler_params=_params(2),
        name="conv_mixer",
    )(x, x, x, mods, gpre, gpost, gffn, w_in, ck, w_out, wr_t)


def _excl_cumsum(mask_f):
    rows, n = mask_f.shape
    c = V7X_LANES
    ri = lax.broadcasted_iota(jnp.int32, (c, c), 0)
    ci = lax.broadcasted_iota(jnp.int32, (c, c), 1)
    tri = jnp.where(ri < ci, 1.0, 0.0).astype(BF16)
    ones = jnp.ones((c, c), BF16)
    carry = jnp.zeros((rows, c), F32)
    outs = []
    for j in range(n // c):
        ch = mask_f[:, j * c:(j + 1) * c].astype(BF16)
        outs.append(_bdot(ch, tri) + carry)
        carry = carry + _bdot(ch, ones)
    return jnp.concatenate(outs, axis=1)


def _topk_kernel(aff_ref, slot_ref, *, cap):
    a = aff_ref[0]
    rows = a.shape[0]
    thr = jnp.zeros((rows, 1), jnp.int32)
    capf = float(cap)
    for bit in range(30, -1, -1):
        cand = thr | (1 << bit)
        cand_f = lax.bitcast_convert_type(cand, F32)
        cnt = jnp.sum(jnp.where(a >= cand_f, 1.0, 0.0), axis=1, keepdims=True)
        thr = jnp.where(cnt >= capf, cand, thr)
    thr_f = lax.bitcast_convert_type(thr, F32)
    gt = jnp.where(a > thr_f, 1.0, 0.0)
    eq = jnp.where(a == thr_f, 1.0, 0.0)
    need = capf - jnp.sum(gt, axis=1, keepdims=True)
    sel = gt + eq * jnp.where(_excl_cumsum(eq) < need, 1.0, 0.0)
    slot = _excl_cumsum(sel)
    slot_ref[0] = jnp.where(sel > 0.0, slot, -1.0).astype(jnp.int32)


def _topk(aff_t, cap):
    bsz, e, n = aff_t.shape
    group = math.gcd(bsz, TOPK_SAMPLES_PER_STEP)
    rows = group * e
    slot = pl.pallas_call(
        functools.partial(_topk_kernel, cap=cap),
        grid=(bsz // group,),
        in_specs=[pl.BlockSpec((1, rows, n), lambda b: (b, 0, 0))],
        out_specs=pl.BlockSpec((1, rows, n), lambda b: (b, 0, 0)),
        out_shape=jax.ShapeDtypeStruct((bsz // group, rows, n), jnp.int32),
        compiler_params=_params(1),
        name="topk",
    )(aff_t.reshape(bsz // group, rows, n))
    return slot.reshape(bsz, e, n)


def _slot_hits(slot_row, cap):
    return slot_row == lax.broadcasted_iota(jnp.int32, (cap, slot_row.shape[1]), 0)


def _gather_rows(hit, h):
    return _bdot(jnp.where(hit, 1.0, 0.0).astype(BF16), h).astype(BF16)


def _gather_vals(hit, aff_row):
    return jnp.sum(jnp.where(hit, aff_row, 0.0), axis=1, keepdims=True)


def _expert_kernel(slot_ref, aff_ref, h_ref, wg_ref, wu_ref, wd_ref, ys_ref,
                   wg_s, wu_s, wd_s, *, cap):
    r = pl.program_id(0)
    b = pl.program_id(1)
    n_exp = pl.num_programs(0) - 1
    fill = r % 2
    rows_in = wg_ref.shape[2]
    rows_dn = wd_ref.shape[2]

    @pl.when(r < n_exp)
    def _():
        o_in = pl.multiple_of(b * rows_in, rows_in)
        o_dn = pl.multiple_of(b * rows_dn, rows_dn)
        wg_s[fill, pl.ds(o_in, rows_in), :] = wg_ref[0, 0].astype(BF16)
        wu_s[fill, pl.ds(o_in, rows_in), :] = wu_ref[0, 0].astype(BF16)
        wd_s[fill, pl.ds(o_dn, rows_dn), :] = wd_ref[0, 0].astype(BF16)

    @pl.when(r == 0)
    def _():
        ys_ref[0, 0] = jnp.zeros(ys_ref.shape[2:], BF16)

    @pl.when(r > 0)
    def _():
        use = 1 - fill
        hit = _slot_hits(slot_ref[0, 0], cap)
        xs = _gather_rows(hit, h_ref[0])
        g = _bdot(xs, wg_s[use])
        u = _bdot(xs, wu_s[use])
        y = _bdot((_silu(g) * u).astype(BF16), wd_s[use])
        ys_ref[0, 0] = (y * _gather_vals(hit, aff_ref[0, 0])).astype(BF16)


def _experts(slot_t, aff_t, h2, wg, wu, wd, layer, cap):
    bsz, e, n = slot_t.shape
    d, f = D_MODEL, EXPERT_FF
    assert d % bsz == 0 and f % bsz == 0
    cur = lambda r, b: (b, jnp.maximum(r - 1, 0), 0, 0)
    nxt = lambda r, b: (layer, jnp.minimum(r, e - 1), b, 0)
    return pl.pallas_call(
        functools.partial(_expert_kernel, cap=cap),
        grid=(e + 1, bsz),
        in_specs=[
            pl.BlockSpec((1, 1, 1, n), cur),
            pl.BlockSpec((1, 1, 1, n), cur),
            pl.BlockSpec((1, n, d), lambda r, b: (b, 0, 0)),
            pl.BlockSpec((1, 1, d // bsz, f), nxt),
            pl.BlockSpec((1, 1, d // bsz, f), nxt),
            pl.BlockSpec((1, 1, f // bsz, d), nxt),
        ],
        out_specs=pl.BlockSpec((1, 1, cap, d),
                               lambda r, b: (b, jnp.where(r == 0, e, r - 1), 0, 0)),
        out_shape=jax.ShapeDtypeStruct((bsz, e + 1, cap, d), BF16),
        scratch_shapes=[pltpu.VMEM((2, d, f), BF16), pltpu.VMEM((2, d, f), BF16),
                        pltpu.VMEM((2, f, d), BF16)],
        compiler_params=pltpu.CompilerParams(
            dimension_semantics=("arbitrary", "arbitrary"), vmem_limit_bytes=VMEM_LIMIT),
        name="experts",
    )(slot_t.reshape(bsz, e, 1, n), aff_t.reshape(bsz, e, 1, n), h2, wg, wu, wd)


def _expert_stream_kernel(slot_ref, aff_ref, h_ref, wg_ref, wu_ref, wd_ref, ys_ref,
                          xs_ref, y_ref, *, cap):
    c = pl.program_id(1)
    bsz = h_ref.shape[0]

    @pl.when(c == 0)
    def _():
        for i in range(bsz):
            xs_ref[i * cap:(i + 1) * cap, :] = _gather_rows(
                _slot_hits(slot_ref[i, 0], cap), h_ref[i])
        y_ref[...] = jnp.zeros(y_ref.shape, F32)

    xs = xs_ref[...]
    g = _bdot(xs, wg_ref[0, 0].astype(BF16))
    u = _bdot(xs, wu_ref[0, 0].astype(BF16))
    y_ref[...] += _bdot((_silu(g) * u).astype(BF16), wd_ref[0, 0].astype(BF16))

    @pl.when(c == pl.num_programs(1) - 1)
    def _():
        for i in range(bsz):
            vals = _gather_vals(_slot_hits(slot_ref[i, 0], cap), aff_ref[i, 0])
            ys_ref[i, 0] = (y_ref[i * cap:(i + 1) * cap, :] * vals).astype(BF16)


def _experts_stream(slot_t, aff_t, h2, wg, wu, wd, layer, cap):
    bsz, e, n = slot_t.shape
    d, f = D_MODEL, EXPERT_FF
    fc = f // CTX_FF_CHUNKS
    row = lambda ei, c: (0, ei, 0, 0)
    return pl.pallas_call(
        functools.partial(_expert_stream_kernel, cap=cap),
        grid=(e, CTX_FF_CHUNKS),
        in_specs=[
            pl.BlockSpec((bsz, 1, 1, n), row),
            pl.BlockSpec((bsz, 1, 1, n), row),
            pl.BlockSpec((bsz, n, d), lambda ei, c: (0, 0, 0)),
            pl.BlockSpec((1, 1, d, fc), lambda ei, c: (layer, ei, 0, c)),
            pl.BlockSpec((1, 1, d, fc), lambda ei, c: (layer, ei, 0, c)),
            pl.BlockSpec((1, 1, fc, d), lambda ei, c: (layer, ei, c, 0)),
        ],
        out_specs=pl.BlockSpec((bsz, 1, cap, d), row),
        out_shape=jax.ShapeDtypeStruct((bsz, e, cap, d), BF16),
        scratch_shapes=[pltpu.VMEM((bsz * cap, d), BF16), pltpu.VMEM((bsz * cap, d), F32)],
        compiler_params=pltpu.CompilerParams(
            dimension_semantics=("parallel", "arbitrary"), vmem_limit_bytes=VMEM_LIMIT),
        name="experts_stream",
    )(slot_t.reshape(bsz, e, 1, n), aff_t.reshape(bsz, e, 1, n), h2, wg, wu, wd)


def _combine_kernel(slot_ref, ys_ref, x_ref, mods_ref, gpost_ref, xo_ref, *, cap):
    e = N_EXPERTS
    sl = slot_ref[0]
    tn = sl.shape[0]
    if cap % V7X_LANES == 0:
        lane = lax.broadcasted_iota(jnp.int32, (1, cap), 1)
        pt = jnp.concatenate(
            [jnp.where(sl[:, j:j + 1] == lane, 1.0, 0.0).astype(BF16) for j in range(e)],
            axis=1)
    else:
        lane = lax.broadcasted_iota(jnp.int32, (1, e * cap), 1)
        acc = jnp.zeros((tn, e * cap), F32)
        for j in range(e):
            col = sl[:, j:j + 1]
            acc = acc + jnp.where(jnp.where(col >= 0, col + j * cap, -1) == lane, 1.0, 0.0)
        pt = acc.astype(BF16)
    f = _bdot(pt, ys_ref[0])
    g2 = mods_ref[0][5:6]
    xo_ref[0] = x_ref[0] + g2 * _rms(f, gpost_ref[...])


def _combine(slot, ys, x, mods, gpost, cap):
    bsz, n, d = x.shape
    e = N_EXPERTS
    tn = min(SEQ_TILE, n)
    return pl.pallas_call(
        functools.partial(_combine_kernel, cap=cap),
        grid=(bsz, n // tn),
        in_specs=[
            pl.BlockSpec((1, tn, e), lambda b, t: (b, t, 0)),
            pl.BlockSpec((1, e * cap, d), lambda b, t: (b, 0, 0)),
            pl.BlockSpec((1, tn, d), lambda b, t: (b, t, 0)),
            _mods_spec(mods),
            _const_spec((1, d)),
        ],
        out_specs=pl.BlockSpec((1, tn, d), lambda b, t: (b, t, 0)),
        out_shape=jax.ShapeDtypeStruct((bsz, n, d), F32),
        compiler_params=_params(2),
        name="combine",
    )(slot, ys.reshape(bsz, -1, d), x, mods, gpost)


def _ec_moe(x, h2, aff_t, mods, gpost, wg, wu, wd, layer, stream):
    n = x.shape[1]
    cap = EC_FACTOR * n // N_EXPERTS
    slot_t = _topk(aff_t, cap)
    experts = _experts_stream if stream else _experts
    ys = experts(slot_t, aff_t, h2, wg, wu, wd, layer, cap)
    slot = jnp.swapaxes(slot_t, 1, 2)
    return _combine(slot, ys, x, mods, gpost, cap)


NT_DIMS = (((1,), (1,)), ((), ()))


def _qkv_kernel(x_ref, mods_ref, gpre_ref, w_ref, wvt_ref, cos_ref, sin_ref, *out_refs,
                with_q, rope):
    d = D_MODEL
    mods = mods_ref[0]
    sh1, sc1 = mods[0:1], mods[1:2]
    h = (_rms(x_ref[0], gpre_ref[...]) * (1.0 + sc1) + sh1).astype(BF16)
    proj = _bdot(h, w_ref[...])
    c = V7X_LANES
    lane = lax.broadcasted_iota(jnp.int32, (1, c), 1)
    first_half = (lane % (HEAD_DIM // 2)) < (HEAD_DIM // 4)
    cos = cos_ref[...]
    sin = sin_ref[...]
    n_rot = 2 if with_q else 1
    for a in range(n_rot):
        scale = HEAD_DIM ** -0.5 * math.log2(math.e) if (with_q and a == 0) else 1.0
        for j in range(d // c):
            ch = proj[:, a * d + j * c: a * d + (j + 1) * c]
            if rope:
                partner = jnp.where(first_half,
                                    pltpu.roll(ch, c - HEAD_DIM // 4, axis=1),
                                    pltpu.roll(ch, HEAD_DIM // 4, axis=1))
                ch = ch * cos + partner * sin
            if scale != 1.0:
                ch = ch * scale
            out_refs[a][0, :, j * c:(j + 1) * c] = ch.astype(BF16)
    out_refs[n_rot][0] = lax.dot_general(wvt_ref[...], h, NT_DIMS,
                                         preferred_element_type=F32).astype(BF16)


def _qkv(x, mods, gpre, w, wvt, cos, sin, with_q, rope):
    bsz, n, d = x.shape
    ts = min(SEQ_TILE, n)
    n_rot = 2 if with_q else 1
    tok = pl.BlockSpec((1, ts, d), lambda b, t: (b, t, 0))
    tab = pl.BlockSpec((ts, V7X_LANES), lambda b, t: (t, 0))
    return pl.pallas_call(
        functools.partial(_qkv_kernel, with_q=with_q, rope=rope),
        grid=(bsz, n // ts),
        in_specs=[tok, _mods_spec(mods), _const_spec((1, d)), _const_spec((d, n_rot * d)),
                  _const_spec((d, d)), tab, tab],
        out_specs=[tok] * n_rot + [pl.BlockSpec((1, d, ts), lambda b, t: (b, 0, t))],
        out_shape=[jax.ShapeDtypeStruct((bsz, n, d), BF16)] * n_rot
        + [jax.ShapeDtypeStruct((bsz, d, n), BF16)],
        compiler_params=_params(2),
        name="qkv",
    )(x, mods, gpre, w, wvt, cos, sin)


def _attn_kernel(q_ref, kl_ref, vlt_ref, kc_ref, vct_ref, lam_ref, g_ref, o_ref, s_ref,
                 *, lam_init):
    lm = lam_ref[...]
    lam = (jnp.exp(jnp.sum(lm[0:1] * lm[1:2], axis=1, keepdims=True))
           - jnp.exp(jnp.sum(lm[2:3] * lm[3:4], axis=1, keepdims=True)) + lam_init)
    n, nc = kl_ref.shape[1], kc_ref.shape[1]
    sub = V7X_SUBLANES
    tq = ATTN_SUB_Q
    segs = [(kc_ref, vct_ref, 0, nc, 0)]
    segs += [(kl_ref, vlt_ref, j, ATTN_K_CHUNK, nc + j) for j in range(0, n, ATTN_K_CHUNK)]
    lane = lax.broadcasted_iota(jnp.int32, (1, V_DIM), 1)

    def score_seg(chain, qc, seg, m8):
        k_ref, _, off, rows, base = seg
        s = lax.dot_general(k_ref[0, off:off + rows, :], qc, NT_DIMS,
                            preferred_element_type=F32)
        s_ref[chain % 2, base:base + rows, :] = s
        part = jnp.max(s.reshape(rows // sub, sub, tq), axis=0)
        return part if m8 is None else jnp.maximum(m8, part)

    def exp_seg(chain, col_max, seg, acc, l8):
        _, vt_ref, off, rows, base = seg
        e = jnp.exp2(s_ref[chain % 2, base:base + rows, :] - col_max)
        l8 = l8 + jnp.sum(e.reshape(rows // sub, sub, tq), axis=0)
        return acc + _bdot(vt_ref[0, :, off:off + rows], e.astype(BF16)), l8

    n_chains = 2 * (q_ref.shape[1] // tq)
    heads = []
    prev = None
    for chain in range(n_chains + 1):
        if chain < n_chains:
            t, comp = divmod(chain, 2)
            in_comp = (lane >= comp * HEAD_DIM) & (lane < (comp + 1) * HEAD_DIM)
            q = q_ref[0, t * tq:(t + 1) * tq, :]
            qc = jnp.where(in_comp, q, jnp.zeros_like(q))
        m8 = None
        acc = jnp.zeros((V_DIM, tq), F32)
        l8 = jnp.zeros((sub, tq), F32)
        for seg in segs:
            if chain < n_chains:
                m8 = score_seg(chain, qc, seg, m8)
            if prev is not None:
                acc, l8 = exp_seg(prev[0], prev[1], seg, acc, l8)
        if prev is not None:
            heads.append(acc / jnp.sum(l8, axis=0, keepdims=True))
            if len(heads) == 2:
                t = prev[0] // 2
                o = jnp.transpose(heads[0] - lam * heads[1])
                o_ref[0, t * tq:(t + 1) * tq, :] = (
                    _rms(o, g_ref[...]) * (1.0 - lam_init)).astype(BF16)
                heads = []
        prev = (chain, jnp.max(m8, axis=0, keepdims=True)) if chain < n_chains else None


def _attention(q, k_l, vt_l, k_c, vt_c, lam_rows, subln_g, lam_init):
    bsz, n, d = q.shape
    nc = k_c.shape[1]
    tq = ATTN_Q_TILE
    hw = V_DIM
    return pl.pallas_call(
        functools.partial(_attn_kernel, lam_init=lam_init),
        grid=(bsz, N_HEADS, n // tq),
        in_specs=[
            pl.BlockSpec((1, tq, hw), lambda b, h, t: (b, t, h)),
            pl.BlockSpec((1, n, hw), lambda b, h, t: (b, 0, h)),
            pl.BlockSpec((1, hw, n), lambda b, h, t: (b, h, 0)),
            pl.BlockSpec((1, nc, hw), lambda b, h, t: (b, 0, h)),
            pl.BlockSpec((1, hw, nc), lambda b, h, t: (b, h, 0)),
            _const_spec((4, HEAD_DIM)),
            _const_spec((1, hw)),
        ],
        out_specs=pl.BlockSpec((1, tq, hw), lambda b, h, t: (b, t, h)),
        out_shape=jax.ShapeDtypeStruct((bsz, n, d), BF16),
        scratch_shapes=[pltpu.VMEM((2, nc + n, ATTN_SUB_Q), F32)],
        compiler_params=_params(3),
        name="diff_attn",
    )(q, k_l, vt_l, k_c, vt_c, lam_rows, subln_g)


def _oproj_kernel(o_ref, x_ref, mods_ref, gpost_ref, gffn_ref, wo_ref, wr_ref,
                  xo_ref, h2_ref, aff_ref):
    m = _bdot(o_ref[0], wo_ref[...])
    _mixer_tail(x_ref[0], m, mods_ref[0], gpost_ref, gffn_ref, wr_ref, xo_ref, h2_ref, aff_ref)


def _oproj(o, x, mods, gpost, gffn, w_o, wr_t):
    bsz, n, d = x.shape
    ts = min(SEQ_TILE, n)
    tok = pl.BlockSpec((1, ts, d), lambda b, t: (b, t, 0))
    out_specs, out_shapes = _tail_out(bsz, n, ts)
    return pl.pallas_call(
        _oproj_kernel,
        grid=(bsz, n // ts),
        in_specs=[tok, tok, _mods_spec(mods), _const_spec((1, d)), _const_spec((1, d)),
                  _const_spec((d, d)), _const_spec((N_EXPERTS, d))],
        out_specs=out_specs,
        out_shape=out_shapes,
        compiler_params=_params(2),
        name="oproj",
    )(o, x, mods, gpost, gffn, w_o, wr_t)


def _rope_tables(n):
    pos = jnp.arange(n)
    row = (pos // GRID_W).astype(F32)
    col = (pos % GRID_W).astype(F32)
    n_freq = HEAD_DIM // 4
    inv = 1.0 / (ROPE_BASE ** (jnp.arange(n_freq, dtype=F32) / n_freq))
    ar = row[:, None] * inv
    ac = col[:, None] * inv
    cos64 = jnp.concatenate([jnp.cos(ar), jnp.cos(ar), jnp.cos(ac), jnp.cos(ac)], axis=1)
    sin64 = jnp.concatenate([-jnp.sin(ar), jnp.sin(ar), -jnp.sin(ac), jnp.sin(ac)], axis=1)
    reps = V7X_LANES // HEAD_DIM
    return jnp.tile(cos64, (1, reps)), jnp.tile(sin64, (1, reps))


def kernel(x, c, ctx, c_ctx, ada_w, ada_b, pre_mix_g, post_mix_g, pre_ffn_g, post_ffn_g,
           conv_w_in, conv_k, conv_w_out, attn_w_qkv, attn_lambda_q1, attn_lambda_k1,
           attn_lambda_q2, attn_lambda_k2, attn_subln_g, attn_w_o, router_w,
           moe_w_gate, moe_w_up, moe_w_down):
    bsz, n, d = x.shape
    depth = ada_w.shape[0]
    assert (depth, d) == (2, D_MODEL) and n % SEQ_TILE == 0 and n % GRID_W == 0

    pad = (-(bsz + 1)) % V7X_SUBLANES
    cond = jnp.concatenate([c, c_ctx[None, :], jnp.zeros((pad, d), F32)], axis=0)
    ada = _ada(cond, ada_w, ada_b)
    mods_l = [ada[i, :bsz].reshape(bsz, N_MOD, d) for i in range(depth)]
    mods_c = [ada[i, bsz:bsz + 1].reshape(1, N_MOD, d) for i in range(depth)]
    row = lambda g, i: g[i].reshape(1, d)
    wr_t = [jnp.swapaxes(router_w[i], 0, 1) for i in range(depth)]
    wg, wu, wd = moe_w_gate, moe_w_up, moe_w_down

    conv_args = (row(pre_mix_g, 0), row(post_mix_g, 0), row(pre_ffn_g, 0),
                 conv_w_in[0].astype(BF16), conv_k[0], conv_w_out[0].astype(BF16), wr_t[0])
    x_l, h2_l, aff_l = _conv_mixer(x, mods_l[0], *conv_args)
    x_c, h2_c, aff_c = _conv_mixer(ctx, mods_c[0], *conv_args)
    gp0 = row(post_ffn_g, 0)
    x_l = _ec_moe(x_l, h2_l, aff_l, mods_l[0], gp0, wg, wu, wd, 0, stream=False)
    x_c = _ec_moe(x_c, h2_c, aff_c, mods_c[0], gp0, wg, wu, wd, 0, stream=True)

    lam_init = 0.8 - 0.6 * math.exp(-0.3 * 1)
    cos, sin = _rope_tables(n)
    w_qkv = attn_w_qkv[0].astype(BF16)
    w_qk, w_k, w_vt = w_qkv[:, :2 * d], w_qkv[:, d:2 * d], jnp.swapaxes(w_qkv[:, 2 * d:], 0, 1)
    gpre1 = row(pre_mix_g, 1)
    q, k_l, vt_l = _qkv(x_l, mods_l[1], gpre1, w_qk, w_vt, cos, sin, with_q=True, rope=True)
    nc = ctx.shape[1]
    k_c, vt_c = _qkv(x_c, mods_c[1], gpre1, w_k, w_vt, cos[:nc], sin[:nc],
                     with_q=False, rope=False)
    lam_rows = jnp.stack([attn_lambda_q1[0], attn_lambda_k1[0],
                          attn_lambda_q2[0], attn_lambda_k2[0]], axis=0)
    o = _attention(q, k_l, vt_l, k_c, vt_c, lam_rows, attn_subln_g[0].reshape(1, V_DIM),
                   lam_init)
    x_l, h2_l, aff_l = _oproj(o, x_l, mods_l[1], row(post_mix_g, 1), row(pre_ffn_g, 1),
                              attn_w_o[0].astype(BF16), wr_t[1])
    return _ec_moe(x_l, h2_l, aff_l, mods_l[1], row(post_ffn_g, 1), wg, wu, wd, 1,
                   stream=False)
```

```python
import functools
import math

import jax
import jax.numpy as jnp
from jax import lax
from jax.experimental import pallas as pl
from jax.experimental.pallas import tpu as pltpu

D_MODEL = 1024
N_EXPERTS = 16
EXPERT_FF = 2048
EC_FACTOR = 2
N_HEADS = 8
HEAD_DIM = 64
V_DIM = 2 * HEAD_DIM
GRID_W = 64
ROPE_BASE = 10000.0
N_MOD = 6
EPS = 1e-6

V7X_LANES = 128
V7X_SUBLANES = 8
V7X_VMEM_BYTES = 64 * 1024 * 1024
VMEM_LIMIT = V7X_VMEM_BYTES - 8 * 1024 * 1024

SEQ_TILE = 512
TOPK_SAMPLES_PER_STEP = 4
CTX_FF_CHUNKS = 4
ATTN_Q_TILE = 2048
ATTN_SUB_Q = 256
ATTN_K_CHUNK = 512
HALO = V7X_SUBLANES

F32 = jnp.float32
BF16 = jnp.bfloat16


def _params(n_axes):
    return pltpu.CompilerParams(
        dimension_semantics=("parallel",) * n_axes, vmem_limit_bytes=VMEM_LIMIT)


def _rms(x, g):
    return x * lax.rsqrt(jnp.mean(x * x, axis=-1, keepdims=True) + EPS) * g


def _silu(x):
    return x * (1.0 / (1.0 + jnp.exp(-x)))


def _bdot(a, b):
    return jnp.dot(a, b, preferred_element_type=F32)


def _ada_kernel(cond_ref, w_ref, b_ref, o_ref):
    s = _silu(cond_ref[...])
    o_ref[0] = jnp.dot(s, w_ref[0], precision=lax.Precision.HIGHEST,
                       preferred_element_type=F32) + b_ref[0]


def _ada(cond, ada_w, ada_b):
    depth, d, nd = ada_w.shape
    rows = cond.shape[0]
    tn = nd // 4
    return pl.pallas_call(
        _ada_kernel,
        grid=(depth, nd // tn),
        in_specs=[
            pl.BlockSpec((rows, d), lambda i, j: (0, 0)),
            pl.BlockSpec((1, d, tn), lambda i, j: (i, 0, j)),
            pl.BlockSpec((1, 1, tn), lambda i, j: (i, 0, j)),
        ],
        out_specs=pl.BlockSpec((1, rows, tn), lambda i, j: (i, 0, j)),
        out_shape=jax.ShapeDtypeStruct((depth, rows, nd), F32),
        compiler_params=_params(2),
        name="ada",
    )(cond, ada_w, ada_b.reshape(depth, 1, nd))


def _mixer_tail(x, m, mods, gpost_ref, gffn_ref, wr_ref, xo_ref, h2_ref, aff_ref):
    g1, sh2, sc2 = mods[2:3], mods[3:4], mods[4:5]
    xn = x + g1 * _rms(m, gpost_ref[...])
    xo_ref[0] = xn
    h2 = _rms(xn, gffn_ref[...]) * (1.0 + sc2) + sh2
    h2b = h2.astype(BF16)
    h2_ref[0] = h2b
    logits = lax.dot_general(wr_ref[...].astype(BF16), h2b, (((1,), (1,)), ((), ())),
                             preferred_element_type=F32)
    ex = jnp.exp(logits - jnp.max(logits, axis=0, keepdims=True))
    aff_ref[0] = ex / jnp.sum(ex, axis=0, keepdims=True)


def _conv_kernel(x_ref, xp_ref, xn_ref, mods_ref, gpre_ref, gpost_ref, gffn_ref,
                 win_ref, ck_ref, wout_ref, wr_ref, xo_ref, h2_ref, aff_ref, *, ts):
    d = D_MODEL
    t = pl.program_id(1)
    last = pl.num_programs(1) - 1
    mods = mods_ref[0]
    sh1, sc1 = mods[0:1], mods[1:2]
    x = x_ref[0]
    rows = ts + 2 * HALO
    xe = jnp.concatenate([xp_ref[0], x, xn_ref[0]], axis=0)
    h = _rms(xe, gpre_ref[...]) * (1.0 + sc1) + sh1
    proj = _bdot(h.astype(BF16), win_ref[...])
    u = proj[:, d:2 * d] * proj[:, 2 * d:]
    r = lax.broadcasted_iota(jnp.int32, (rows, 1), 0)
    inside = jnp.logical_and(jnp.logical_or(r >= HALO, t > 0),
                             jnp.logical_or(r < ts + HALO, t < last))
    u = jnp.where(inside, u, 0.0)
    ck = ck_ref[...]
    u_prev = pltpu.roll(u, 1, axis=0)[HALO:HALO + ts]
    u_next = pltpu.roll(u, rows - 1, axis=0)[HALO:HALO + ts]
    conv = ck[0:1] * u_prev + ck[1:2] * u[HALO:HALO + ts] + ck[2:3] * u_next
    z = proj[HALO:HALO + ts, :d] * conv
    m = _bdot(z.astype(BF16), wout_ref[...])
    _mixer_tail(x, m, mods, gpost_ref, gffn_ref, wr_ref, xo_ref, h2_ref, aff_ref)


def _const_spec(shape):
    return pl.BlockSpec(shape, lambda *_: (0,) * len(shape))


def _mods_spec(mods):
    if mods.shape[0] == 1:
        return pl.BlockSpec((1, N_MOD, D_MODEL), lambda b, t: (0, 0, 0))
    return pl.BlockSpec((1, N_MOD, D_MODEL), lambda b, t: (b, 0, 0))


def _tail_out(bsz, n, ts):
    d, e = D_MODEL, N_EXPERTS
    specs = [
        pl.BlockSpec((1, ts, d), lambda b, t: (b, t, 0)),
        pl.BlockSpec((1, ts, d), lambda b, t: (b, t, 0)),
        pl.BlockSpec((1, e, ts), lambda b, t: (b, 0, t)),
    ]
    shapes = [
        jax.ShapeDtypeStruct((bsz, n, d), F32),
        jax.ShapeDtypeStruct((bsz, n, d), BF16),
        jax.ShapeDtypeStruct((bsz, e, n), F32),
    ]
    return specs, shapes


def _conv_mixer(x, mods, gpre, gpost, gffn, w_in, ck, w_out, wr_t):
    bsz, n, d = x.shape
    ts = min(SEQ_TILE, n)
    nblk = n // HALO
    per = ts // HALO
    out_specs, out_shapes = _tail_out(bsz, n, ts)
    return pl.pallas_call(
        functools.partial(_conv_kernel, ts=ts),
        grid=(bsz, n // ts),
        in_specs=[
            pl.BlockSpec((1, ts, d), lambda b, t: (b, t, 0)),
            pl.BlockSpec((1, HALO, d), lambda b, t: (b, jnp.maximum(t * per - 1, 0), 0)),
            pl.BlockSpec((1, HALO, d), lambda b, t: (b, jnp.minimum((t + 1) * per, nblk - 1), 0)),
            _mods_spec(mods),
            _const_spec((1, d)), _const_spec((1, d)), _const_spec((1, d)),
            _const_spec((d, 3 * d)),
            _const_spec((3, d)),
            _const_spec((d, d)),
            _const_spec((N_EXPERTS, d)),
        ],
        out_specs=out_specs,
        out_shape=out_shapes,
        compiler_params=_params(2),
        name="conv_mixer",
    )(x, x, x, mods, gpre, gpost, gffn, w_in, ck, w_out, wr_t)


def _excl_cumsum(mask_f):
    rows, n = mask_f.shape
    c = V7X_LANES
    ri = lax.broadcasted_iota(jnp.int32, (c, c), 0)
    ci = lax.broadcasted_iota(jnp.int32, (c, c), 1)
    tri = jnp.where(ri < ci, 1.0, 0.0).astype(BF16)
    ones = jnp.ones((c, c), BF16)
    carry = jnp.zeros((rows, c), F32)
    outs = []
    for j in range(n // c):
        ch = mask_f[:, j * c:(j + 1) * c].astype(BF16)
        outs.append(_bdot(ch, tri) + carry)
        carry = carry + _bdot(ch, ones)
    return jnp.concatenate(outs, axis=1)


def _topk_kernel(aff_ref, slot_ref, *, cap):
    a = aff_ref[0]
    rows = a.shape[0]
    thr = jnp.zeros((rows, 1), jnp.int32)
    capf = float(cap)
    for bit in range(30, -1, -1):
        cand = thr | (1 << bit)
        cand_f = lax.bitcast_convert_type(cand, F32)
        cnt = jnp.sum(jnp.where(a >= cand_f, 1.0, 0.0), axis=1, keepdims=True)
        thr = jnp.where(cnt >= capf, cand, thr)
    thr_f = lax.bitcast_convert_type(thr, F32)
    gt = jnp.where(a > thr_f, 1.0, 0.0)
    eq = jnp.where(a == thr_f, 1.0, 0.0)
    need = capf - jnp.sum(gt, axis=1, keepdims=True)
    sel = gt + eq * jnp.where(_excl_cumsum(eq) < need, 1.0, 0.0)
    slot = _excl_cumsum(sel)
    slot_ref[0] = jnp.where(sel > 0.0, slot, -1.0).astype(jnp.int32)


def _topk(aff_t, cap):
    bsz, e, n = aff_t.shape
    group = math.gcd(bsz, TOPK_SAMPLES_PER_STEP)
    rows = group * e
    slot = pl.pallas_call(
        functools.partial(_topk_kernel, cap=cap),
        grid=(bsz // group,),
        in_specs=[pl.BlockSpec((1, rows, n), lambda b: (b, 0, 0))],
        out_specs=pl.BlockSpec((1, rows, n), lambda b: (b, 0, 0)),
        out_shape=jax.ShapeDtypeStruct((bsz // group, rows, n), jnp.int32),
        compiler_params=_params(1),
        name="topk",
    )(aff_t.reshape(bsz // group, rows, n))
    return slot.reshape(bsz, e, n)


def _slot_hits(slot_row, cap):
    return slot_row == lax.broadcasted_iota(jnp.int32, (cap, slot_row.shape[1]), 0)


def _gather_rows(hit, h):
    return _bdot(jnp.where(hit, 1.0, 0.0).astype(BF16), h).astype(BF16)


def _gather_vals(hit, aff_row):
    return jnp.sum(jnp.where(hit, aff_row, 0.0), axis=1, keepdims=True)


def _expert_kernel(slot_ref, aff_ref, h_ref, wg_ref, wu_ref, wd_ref, ys_ref,
                   wg_s, wu_s, wd_s, *, cap):
    r = pl.program_id(0)
    b = pl.program_id(1)
    n_exp = pl.num_programs(0) - 1
    fill = r % 2
    rows_in = wg_ref.shape[2]
    rows_dn = wd_ref.shape[2]

    @pl.when(r < n_exp)
    def _():
        o_in = pl.multiple_of(b * rows_in, rows_in)
        o_dn = pl.multiple_of(b * rows_dn, rows_dn)
        wg_s[fill, pl.ds(o_in, rows_in), :] = wg_ref[0, 0].astype(BF16)
        wu_s[fill, pl.ds(o_in, rows_in), :] = wu_ref[0, 0].astype(BF16)
        wd_s[fill, pl.ds(o_dn, rows_dn), :] = wd_ref[0, 0].astype(BF16)

    @pl.when(r == 0)
    def _():
        ys_ref[0, 0] = jnp.zeros(ys_ref.shape[2:], BF16)

    @pl.when(r > 0)
    def _():
        use = 1 - fill
        hit = _slot_hits(slot_ref[0, 0], cap)
        xs = _gather_rows(hit, h_ref[0])
        g = _bdot(xs, wg_s[use])
        u = _bdot(xs, wu_s[use])
        y = _bdot((_silu(g) * u).astype(BF16), wd_s[use])
        ys_ref[0, 0] = (y * _gather_vals(hit, aff_ref[0, 0])).astype(BF16)


def _experts(slot_t, aff_t, h2, wg, wu, wd, layer, cap):
    bsz, e, n = slot_t.shape
    d, f = D_MODEL, EXPERT_FF
    assert d % bsz == 0 and f % bsz == 0
    cur = lambda r, b: (b, jnp.maximum(r - 1, 0), 0, 0)
    nxt = lambda r, b: (layer, jnp.minimum(r, e - 1), b, 0)
    return pl.pallas_call(
        functools.partial(_expert_kernel, cap=cap),
        grid=(e + 1, bsz),
        in_specs=[
            pl.BlockSpec((1, 1, 1, n), cur),
            pl.BlockSpec((1, 1, 1, n), cur),
            pl.BlockSpec((1, n, d), lambda r, b: (b, 0, 0)),
            pl.BlockSpec((1, 1, d // bsz, f), nxt),
            pl.BlockSpec((1, 1, d // bsz, f), nxt),
            pl.BlockSpec((1, 1, f // bsz, d), nxt),
        ],
        out_specs=pl.BlockSpec((1, 1, cap, d),
                               lambda r, b: (b, jnp.where(r == 0, e, r - 1), 0, 0)),
        out_shape=jax.ShapeDtypeStruct((bsz, e + 1, cap, d), BF16),
        scratch_shapes=[pltpu.VMEM((2, d, f), BF16), pltpu.VMEM((2, d, f), BF16),
                        pltpu.VMEM((2, f, d), BF16)],
        compiler_params=pltpu.CompilerParams(
            dimension_semantics=("arbitrary", "arbitrary"), vmem_limit_bytes=VMEM_LIMIT),
        name="experts",
    )(slot_t.reshape(bsz, e, 1, n), aff_t.reshape(bsz, e, 1, n), h2, wg, wu, wd)


def _expert_stream_kernel(slot_ref, aff_ref, h_ref, wg_ref, wu_ref, wd_ref, ys_ref,
                          xs_ref, y_ref, *, cap):
    c = pl.program_id(1)
    bsz = h_ref.shape[0]

    @pl.when(c == 0)
    def _():
        for i in range(bsz):
            xs_ref[i * cap:(i + 1) * cap, :] = _gather_rows(
                _slot_hits(slot_ref[i, 0], cap), h_ref[i])
        y_ref[...] = jnp.zeros(y_ref.shape, F32)

    xs = xs_ref[...]
    g = _bdot(xs, wg_ref[0, 0].astype(BF16))
    u = _bdot(xs, wu_ref[0, 0].astype(BF16))
    y_ref[...] += _bdot((_silu(g) * u).astype(BF16), wd_ref[0, 0].astype(BF16))

    @pl.when(c == pl.num_programs(1) - 1)
    def _():
        for i in range(bsz):
            vals = _gather_vals(_slot_hits(slot_ref[i, 0], cap), aff_ref[i, 0])
            ys_ref[i, 0] = (y_ref[i * cap:(i + 1) * cap, :] * vals).astype(BF16)


def _experts_stream(slot_t, aff_t, h2, wg, wu, wd, layer, cap):
    bsz, e, n = slot_t.shape
    d, f = D_MODEL, EXPERT_FF
    fc = f // CTX_FF_CHUNKS
    row = lambda ei, c: (0, ei, 0, 0)
    return pl.pallas_call(
        functools.partial(_expert_stream_kernel, cap=cap),
        grid=(e, CTX_FF_CHUNKS),
        in_specs=[
            pl.BlockSpec((bsz, 1, 1, n), row),
            pl.BlockSpec((bsz, 1, 1, n), row),
            pl.BlockSpec((bsz, n, d), lambda ei, c: (0, 0, 0)),
            pl.BlockSpec((1, 1, d, fc), lambda ei, c: (layer, ei, 0, c)),
            pl.BlockSpec((1, 1, d, fc), lambda ei, c: (layer, ei, 0, c)),
            pl.BlockSpec((1, 1, fc, d), lambda ei, c: (layer, ei, c, 0)),
        ],
        out_specs=pl.BlockSpec((bsz, 1, cap, d), row),
        out_shape=jax.ShapeDtypeStruct((bsz, e, cap, d), BF16),
        scratch_shapes=[pltpu.VMEM((bsz * cap, d), BF16), pltpu.VMEM((bsz * cap, d), F32)],
        compiler_params=pltpu.CompilerParams(
            dimension_semantics=("parallel", "arbitrary"), vmem_limit_bytes=VMEM_LIMIT),
        name="experts_stream",
    )(slot_t.reshape(bsz, e, 1, n), aff_t.reshape(bsz, e, 1, n), h2, wg, wu, wd)


def _combine_kernel(slot_ref, ys_ref, x_ref, mods_ref, gpost_ref, xo_ref, *, cap):
    e = N_EXPERTS
    sl = slot_ref[0]
    tn = sl.shape[0]
    if cap % V7X_LANES == 0:
        lane = lax.broadcasted_iota(jnp.int32, (1, cap), 1)
        pt = jnp.concatenate(
            [jnp.where(sl[:, j:j + 1] == lane, 1.0, 0.0).astype(BF16) for j in range(e)],
            axis=1)
    else:
        lane = lax.broadcasted_iota(jnp.int32, (1, e * cap), 1)
        acc = jnp.zeros((tn, e * cap), F32)
        for j in range(e):
            col = sl[:, j:j + 1]
            acc = acc + jnp.where(jnp.where(col >= 0, col + j * cap, -1) == lane, 1.0, 0.0)
        pt = acc.astype(BF16)
    f = _bdot(pt, ys_ref[0])
    g2 = mods_ref[0][5:6]
    xo_ref[0] = x_ref[0] + g2 * _rms(f, gpost_ref[...])


def _combine(slot, ys, x, mods, gpost, cap):
    bsz, n, d = x.shape
    e = N_EXPERTS
    tn = min(SEQ_TILE, n)
    return pl.pallas_call(
        functools.partial(_combine_kernel, cap=cap),
        grid=(bsz, n // tn),
        in_specs=[
            pl.BlockSpec((1, tn, e), lambda b, t: (b, t, 0)),
            pl.BlockSpec((1, e * cap, d), lambda b, t: (b, 0, 0)),
            pl.BlockSpec((1, tn, d), lambda b, t: (b, t, 0)),
            _mods_spec(mods),
            _const_spec((1, d)),
        ],
        out_specs=pl.BlockSpec((1, tn, d), lambda b, t: (b, t, 0)),
        out_shape=jax.ShapeDtypeStruct((bsz, n, d), F32),
        compiler_params=_params(2),
        name="combine",
    )(slot, ys.reshape(bsz, -1, d), x, mods, gpost)


def _ec_moe(x, h2, aff_t, mods, gpost, wg, wu, wd, layer, stream):
    n = x.shape[1]
    cap = EC_FACTOR * n // N_EXPERTS
    slot_t = _topk(aff_t, cap)
    experts = _experts_stream if stream else _experts
    ys = experts(slot_t, aff_t, h2, wg, wu, wd, layer, cap)
    slot = jnp.swapaxes(slot_t, 1, 2)
    return _combine(slot, ys, x, mods, gpost, cap)


NT_DIMS = (((1,), (1,)), ((), ()))


def _qkv_kernel(x_ref, mods_ref, gpre_ref, w_ref, wvt_ref, cos_ref, sin_ref, *out_refs,
                with_q, rope):
    d = D_MODEL
    mods = mods_ref[0]
    sh1, sc1 = mods[0:1], mods[1:2]
    h = (_rms(x_ref[0], gpre_ref[...]) * (1.0 + sc1) + sh1).astype(BF16)
    proj = _bdot(h, w_ref[...])
    c = V7X_LANES
    lane = lax.broadcasted_iota(jnp.int32, (1, c), 1)
    first_half = (lane % (HEAD_DIM // 2)) < (HEAD_DIM // 4)
    cos = cos_ref[...]
    sin = sin_ref[...]
    n_rot = 2 if with_q else 1
    for a in range(n_rot):
        scale = HEAD_DIM ** -0.5 * math.log2(math.e) if (with_q and a == 0) else 1.0
        for j in range(d // c):
            ch = proj[:, a * d + j * c: a * d + (j + 1) * c]
            if rope:
                partner = jnp.where(first_half,
                                    pltpu.roll(ch, c - HEAD_DIM // 4, axis=1),
                                    pltpu.roll(ch, HEAD_DIM // 4, axis=1))
                ch = ch * cos + partner * sin
            if scale != 1.0:
                ch = ch * scale
            out_refs[a][0, :, j * c:(j + 1) * c] = ch.astype(BF16)
    out_refs[n_rot][0] = lax.dot_general(wvt_ref[...], h, NT_DIMS,
                                         preferred_element_type=F32).astype(BF16)


def _qkv(x, mods, gpre, w, wvt, cos, sin, with_q, rope):
    bsz, n, d = x.shape
    ts = min(SEQ_TILE, n)
    n_rot = 2 if with_q else 1
    tok = pl.BlockSpec((1, ts, d), lambda b, t: (b, t, 0))
    tab = pl.BlockSpec((ts, V7X_LANES), lambda b, t: (t, 0))
    return pl.pallas_call(
        functools.partial(_qkv_kernel, with_q=with_q, rope=rope),
        grid=(bsz, n // ts),
        in_specs=[tok, _mods_spec(mods), _const_spec((1, d)), _const_spec((d, n_rot * d)),
                  _const_spec((d, d)), tab, tab],
        out_specs=[tok] * n_rot + [pl.BlockSpec((1, d, ts), lambda b, t: (b, 0, t))],
        out_shape=[jax.ShapeDtypeStruct((bsz, n, d), BF16)] * n_rot
        + [jax.ShapeDtypeStruct((bsz, d, n), BF16)],
        compiler_params=_params(2),
        name="qkv",
    )(x, mods, gpre, w, wvt, cos, sin)


def _attn_kernel(q_ref, kl_ref, vlt_ref, kc_ref, vct_ref, lam_ref, g_ref, o_ref, s_ref,
                 *, lam_init):
    lm = lam_ref[...]
    lam = (jnp.exp(jnp.sum(lm[0:1] * lm[1:2], axis=1, keepdims=True))
           - jnp.exp(jnp.sum(lm[2:3] * lm[3:4], axis=1, keepdims=True)) + lam_init)
    n, nc = kl_ref.shape[1], kc_ref.shape[1]
    sub = V7X_SUBLANES
    tq = ATTN_SUB_Q
    segs = [(kc_ref, vct_ref, 0, nc, 0)]
    segs += [(kl_ref, vlt_ref, j, ATTN_K_CHUNK, nc + j) for j in range(0, n, ATTN_K_CHUNK)]
    lane = lax.broadcasted_iota(jnp.int32, (1, V_DIM), 1)

    def score_seg(chain, qc, seg, m8):
        k_ref, _, off, rows, base = seg
        s = lax.dot_general(k_ref[0, off:off + rows, :], qc, NT_DIMS,
                            preferred_element_type=F32)
        s_ref[chain % 2, base:base + rows, :] = s
        part = jnp.max(s.reshape(rows // sub, sub, tq), axis=0)
        return part if m8 is None else jnp.maximum(m8, part)

    def exp_seg(chain, col_max, seg, acc, l8):
        _, vt_ref, off, rows, base = seg
        e = jnp.exp2(s_ref[chain % 2, base:base + rows, :] - col_max)
        l8 = l8 + jnp.sum(e.reshape(rows // sub, sub, tq), axis=0)
        return acc + _bdot(vt_ref[0, :, off:off + rows], e.astype(BF16)), l8

    n_chains = 2 * (q_ref.shape[1] // tq)
    heads = []
    prev = None
    for chain in range(n_chains + 1):
        if chain < n_chains:
            t, comp = divmod(chain, 2)
            in_comp = (lane >= comp * HEAD_DIM) & (lane < (comp + 1) * HEAD_DIM)
            q = q_ref[0, t * tq:(t + 1) * tq, :]
            qc = jnp.where(in_comp, q, jnp.zeros_like(q))
        m8 = None
        acc = jnp.zeros((V_DIM, tq), F32)
        l8 = jnp.zeros((sub, tq), F32)
        for seg in segs:
            if chain < n_chains:
                m8 = score_seg(chain, qc, seg, m8)
            if prev is not None:
                acc, l8 = exp_seg(prev[0], prev[1], seg, acc, l8)
        if prev is not None:
            heads.append(acc / jnp.sum(l8, axis=0, keepdims=True))
            if len(heads) == 2:
                t = prev[0] // 2
                o = jnp.transpose(heads[0] - lam * heads[1])
                o_ref[0, t * tq:(t + 1) * tq, :] = (
                    _rms(o, g_ref[...]) * (1.0 - lam_init)).astype(BF16)
                heads = []
        prev = (chain, jnp.max(m8, axis=0, keepdims=True)) if chain < n_chains else None


def _attention(q, k_l, vt_l, k_c, vt_c, lam_rows, subln_g, lam_init):
    bsz, n, d = q.shape
    nc = k_c.shape[1]
    tq = ATTN_Q_TILE
    hw = V_DIM
    return pl.pallas_call(
        functools.partial(_attn_kernel, lam_init=lam_init),
        grid=(bsz, N_HEADS, n // tq),
        in_specs=[
            pl.BlockSpec((1, tq, hw), lambda b, h, t: (b, t, h)),
            pl.BlockSpec((1, n, hw), lambda b, h, t: (b, 0, h)),
            pl.BlockSpec((1, hw, n), lambda b, h, t: (b, h, 0)),
            pl.BlockSpec((1, nc, hw), lambda b, h, t: (b, 0, h)),
            pl.BlockSpec((1, hw, nc), lambda b, h, t: (b, h, 0)),
            _const_spec((4, HEAD_DIM)),
            _const_spec((1, hw)),
        ],
        out_specs=pl.BlockSpec((1, tq, hw), lambda b, h, t: (b, t, h)),
        out_shape=jax.ShapeDtypeStruct((bsz, n, d), BF16),
        scratch_shapes=[pltpu.VMEM((2, nc + n, ATTN_SUB_Q), F32)],
        compiler_params=_params(3),
        name="diff_attn",
    )(q, k_l, vt_l, k_c, vt_c, lam_rows, subln_g)


def _oproj_kernel(o_ref, x_ref, mods_ref, gpost_ref, gffn_ref, wo_ref, wr_ref,
                  xo_ref, h2_ref, aff_ref):
    m = _bdot(o_ref[0], wo_ref[...])
    _mixer_tail(x_ref[0], m, mods_ref[0], gpost_ref, gffn_ref, wr_ref, xo_ref, h2_ref, aff_ref)


def _oproj(o, x, mods, gpost, gffn, w_o, wr_t):
    bsz, n, d = x.shape
    ts = min(SEQ_TILE, n)
    tok = pl.BlockSpec((1, ts, d), lambda b, t: (b, t, 0))
    out_specs, out_shapes = _tail_out(bsz, n, ts)
    return pl.pallas_call(
        _oproj_kernel,
        grid=(bsz, n // ts),
        in_specs=[tok, tok, _mods_spec(mods), _const_spec((1, d)), _const_spec((1, d)),
                  _const_spec((d, d)), _const_spec((N_EXPERTS, d))],
        out_specs=out_specs,
        out_shape=out_shapes,
        compiler_params=_params(2),
        name="oproj",
    )(o, x, mods, gpost, gffn, w_o, wr_t)


def _rope_tables(n):
    pos = jnp.arange(n)
    row = (pos // GRID_W).astype(F32)
    col = (pos % GRID_W).astype(F32)
    n_freq = HEAD_DIM // 4
    inv = 1.0 / (ROPE_BASE ** (jnp.arange(n_freq, dtype=F32) / n_freq))
    ar = row[:, None] * inv
    ac = col[:, None] * inv
    cos64 = jnp.concatenate([jnp.cos(ar), jnp.cos(ar), jnp.cos(ac), jnp.cos(ac)], axis=1)
    sin64 = jnp.concatenate([-jnp.sin(ar), jnp.sin(ar), -jnp.sin(ac), jnp.sin(ac)], axis=1)
    reps = V7X_LANES // HEAD_DIM
    return jnp.tile(cos64, (1, reps)), jnp.tile(sin64, (1, reps))


def kernel(x, c, ctx, c_ctx, ada_w, ada_b, pre_mix_g, post_mix_g, pre_ffn_g, post_ffn_g,
           conv_w_in, conv_k, conv_w_out, attn_w_qkv, attn_lambda_q1, attn_lambda_k1,
           attn_lambda_q2, attn_lambda_k2, attn_subln_g, attn_w_o, router_w,
           moe_w_gate, moe_w_up, moe_w_down):
    bsz, n, d = x.shape
    depth = ada_w.shape[0]
    assert (depth, d) == (2, D_MODEL) and n % SEQ_TILE == 0 and n % GRID_W == 0

    pad = (-(bsz + 1)) % V7X_SUBLANES
    cond = jnp.concatenate([c, c_ctx[None, :], jnp.zeros((pad, d), F32)], axis=0)
    ada = _ada(cond, ada_w, ada_b)
    mods_l = [ada[i, :bsz].reshape(bsz, N_MOD, d) for i in range(depth)]
    mods_c = [ada[i, bsz:bsz + 1].reshape(1, N_MOD, d) for i in range(depth)]
    row = lambda g, i: g[i].reshape(1, d)
    wr_t = [jnp.swapaxes(router_w[i], 0, 1) for i in range(depth)]
    wg, wu, wd = moe_w_gate, moe_w_up, moe_w_down

    conv_args = (row(pre_mix_g, 0), row(post_mix_g, 0), row(pre_ffn_g, 0),
                 conv_w_in[0].astype(BF16), conv_k[0], conv_w_out[0].astype(BF16), wr_t[0])
    x_l, h2_l, aff_l = _conv_mixer(x, mods_l[0], *conv_args)
    x_c, h2_c, aff_c = _conv_mixer(ctx, mods_c[0], *conv_args)
    gp0 = row(post_ffn_g, 0)
    x_l = _ec_moe(x_l, h2_l, aff_l, mods_l[0], gp0, wg, wu, wd, 0, stream=False)
    x_c = _ec_moe(x_c, h2_c, aff_c, mods_c[0], gp0, wg, wu, wd, 0, stream=True)

    lam_init = 0.8 - 0.6 * math.exp(-0.3 * 1)
    cos, sin = _rope_tables(n)
    w_qkv = attn_w_qkv[0].astype(BF16)
    w_qk, w_k, w_vt = w_qkv[:, :2 * d], w_qkv[:, d:2 * d], jnp.swapaxes(w_qkv[:, 2 * d:], 0, 1)
    gpre1 = row(pre_mix_g, 1)
    q, k_l, vt_l = _qkv(x_l, mods_l[1], gpre1, w_qk, w_vt, cos, sin, with_q=True, rope=True)
    nc = ctx.shape[1]
    k_c, vt_c = _qkv(x_c, mods_c[1], gpre1, w_k, w_vt, cos[:nc], sin[:nc],
                     with_q=False, rope=False)
    lam_rows = jnp.stack([attn_lambda_q1[0], attn_lambda_k1[0],
                          attn_lambda_q2[0], attn_lambda_k2[0]], axis=0)
    o = _attention(q, k_l, vt_l, k_c, vt_c, lam_rows, attn_subln_g[0].reshape(1, V_DIM),
                   lam_init)
    x_l, h2_l, aff_l = _oproj(o, x_l, mods_l[1], row(post_mix_g, 1), row(pre_ffn_g, 1),
                              attn_w_o[0].astype(BF16), wr_t[1])
    return _ec_moe(x_l, h2_l, aff_l, mods_l[1], row(post_ffn_g, 1), wg, wu, wd, 1,
                   stream=False)
```
